```python
import jax
import jax.numpy as jnp
from jax import lax
import numpy as np


D_MODEL = 1024
BATCH = 8
SEQ = 2048
DEPTH = 2

CTX_LEN = 256
GRID_W = 64
N_MIXERS = 2
EPS = 1e-6
N_MOD = 6
POOL_WINDOWS = (2, 4, 8, 16)
POOL_GROUPS = 4
POOL_GROUP_DIM = D_MODEL // POOL_GROUPS
MLSTM_HEADS = 8
MLSTM_V_DIM = D_MODEL // MLSTM_HEADS
MLSTM_QK_DIM = MLSTM_V_DIM // 2
MLSTM_QK_WIDTH = MLSTM_HEADS * MLSTM_QK_DIM
MLSTM_V_WIDTH = MLSTM_HEADS * MLSTM_V_DIM
MLSTM_IN_WIDTH = 2 * MLSTM_QK_WIDTH + 2 * MLSTM_V_WIDTH + 4 * MLSTM_HEADS
MLSTM_CHUNK = 64
D_FF = ((8 * D_MODEL // 3 + 255) // 256) * 256
CONV_K = 3
N_POOL_LAYERS = (DEPTH + 1) // 2
N_MLSTM_LAYERS = DEPTH // 2

kernel_name = 'hybrid_pool_mlstm_dit_block'


def _rms_norm(x, g):
    xf = x.astype(jnp.float32)
    y = xf * lax.rsqrt(jnp.mean(xf * xf, axis=-1, keepdims=True) + EPS)
    return (y * g.astype(jnp.float32)).astype(x.dtype)


def _modulation(cond, w, b):
    m = jax.nn.silu(cond) @ w + b
    return jnp.split(m[:, None, :], N_MOD, axis=-1)


def _modulate(h, g, shift, scale):
    return _rms_norm(h, g) * (1 + scale) + shift


def _dwconv_grid(x, w, b, grid):
    B, N, F = x.shape
    R, W = grid
    y = lax.conv_general_dilated(x.reshape(B, R, W, F), w.astype(x.dtype), (1, 1), 'SAME',
                                 dimension_numbers=('NHWC', 'HWIO', 'NHWC'), feature_group_count=F)
    return y.reshape(B, N, F) + b


def _pool_mixer(h, w_groups, scale, grid):
    B, N, D = h.shape
    R, W = grid
    hf = h.astype(jnp.float32).reshape(B, R, W, POOL_GROUPS, POOL_GROUP_DIM)
    csum = jnp.cumsum(hf, axis=2)
    csum = jnp.pad(csum, ((0, 0), (0, 0), (1, 0), (0, 0), (0, 0)))
    pos = np.arange(W)
    outs = []
    for gi, win in enumerate(POOL_WINDOWS):
        lo = np.clip(pos - win // 2, 0, W)
        hi = np.clip(pos - win // 2 + win, 0, W)
        inv_cnt = (1.0 / (hi - lo)).astype(np.float32)[:, None]
        cg = csum[:, :, :, gi]
        mean = (jnp.take(cg, hi, axis=2) - jnp.take(cg, lo, axis=2)) * inv_cnt
        outs.append(mean - hf[:, :, :, gi])
    y = jnp.stack(outs, axis=3).astype(h.dtype)
    y = jnp.einsum('brwgi,gio->brwgo', y, w_groups)
    return y.reshape(B, N, D) * scale


def _mlstm_inputs(h, w_in, b_gate, conv_w, conv_b, grid):
    B, N, _ = h.shape
    z = h @ w_in
    qk, v, o, gates = jnp.split(z, [2 * MLSTM_QK_WIDTH, 2 * MLSTM_QK_WIDTH + MLSTM_V_WIDTH,
                                    2 * MLSTM_QK_WIDTH + 2 * MLSTM_V_WIDTH], axis=-1)
    qk = jax.nn.silu(_dwconv_grid(qk, conv_w, conv_b, grid))
    q, k = jnp.split(qk, 2, axis=-1)

    def heads(a, d):
        return a.reshape(B, N, MLSTM_HEADS, d).transpose(0, 2, 1, 3).astype(jnp.float32)

    q = heads(q, MLSTM_QK_DIM)
    k = heads(k, MLSTM_QK_DIM) * (MLSTM_QK_DIM ** -0.5)
    v = heads(v, MLSTM_V_DIM)
    gates = (gates + b_gate).astype(jnp.float32).reshape(B, N, 4, MLSTM_HEADS).transpose(2, 0, 3, 1)
    return (q, k, v, o, gates[0], gates[1],
            jax.nn.log_sigmoid(gates[2]), jax.nn.log_sigmoid(gates[3]))


def _mlstm_chunkwise(q, k, v, ig, lf, state0, with_output):
    B, H, T, dk = q.shape
    dv = v.shape[-1]
    L = MLSTM_CHUNK
    nc = T // L
    q = q.reshape(B, H, nc, L, dk)
    k = k.reshape(B, H, nc, L, dk)
    v = v.reshape(B, H, nc, L, dv)
    ig = ig.reshape(B, H, nc, L)
    b = jnp.cumsum(lf.reshape(B, H, nc, L), axis=-1)
    b_last = b[..., -1]
    g = b_last[..., None] - b + ig
    g_max = jnp.max(g, axis=-1)
    w_end = jnp.exp(g - g_max[..., None])
    c_chunk = jnp.einsum('bhcsk,bhcsv,bhcs->bhckv', k, v, w_end)
    n_chunk = jnp.einsum('bhcsk,bhcs->bhck', k, w_end)

    def step(carry, xs):
        C, n, m = carry
        bl, gm, cc, ncnk = xs
        m_new = jnp.maximum(bl + m, gm)
        a = jnp.exp(bl + m - m_new)
        e = jnp.exp(gm - m_new)
        C_new = a[..., None, None] * C + e[..., None, None] * cc
        n_new = a[..., None] * n + e[..., None] * ncnk
        return (C_new, n_new, m_new), (C, n, m)

    xs = (jnp.moveaxis(b_last, -1, 0), jnp.moveaxis(g_max, -1, 0),
          jnp.moveaxis(c_chunk, 2, 0), jnp.moveaxis(n_chunk, 2, 0))
    state_f, (c_prev, n_prev, m_prev) = lax.scan(step, state0, xs)
    if not with_output:
        return None, state_f
    c_prev = jnp.moveaxis(c_prev, 0, 2)
    n_prev = jnp.moveaxis(n_prev, 0, 2)
    m_prev = jnp.moveaxis(m_prev, 0, -1)
    mask = np.tril(np.ones((L, L), dtype=bool))
    log_d = jnp.where(mask, b[..., :, None] - b[..., None, :] + ig[..., None, :], -jnp.inf)
    inter_log = b + m_prev[..., None]
    m_t = jnp.maximum(inter_log, jnp.max(log_d, axis=-1))
    d_w = jnp.exp(log_d - m_t[..., None])
    a_t = jnp.exp(inter_log - m_t)
    s = jnp.einsum('bhctk,bhcsk->bhcts', q, k) * d_w
    num = jnp.einsum('bhcts,bhcsv->bhctv', s, v) + a_t[..., None] * jnp.einsum('bhctk,bhckv->bhctv', q, c_prev)
    den = jnp.sum(s, axis=-1) + a_t * jnp.einsum('bhctk,bhck->bhct', q, n_prev)
    h = num / jnp.maximum(jnp.abs(den), jnp.exp(-m_t))[..., None]
    return h.reshape(B, H, T, dv), state_f


def _mlstm_mixer(h, w_in, b_gate, conv_w, conv_b, w_out, grid, state_f0, state_b0, with_output):
    q, k, v, o, ig_f, ig_b, lf_f, lf_b = _mlstm_inputs(h, w_in, b_gate, conv_w, conv_b, grid)
    h_f, st_f = _mlstm_chunkwise(q, k, v, ig_f, lf_f, state_f0, with_output)
    flip = lambda a: jnp.flip(a, axis=2)
    h_b, st_b = _mlstm_chunkwise(flip(q), flip(k), flip(v), flip(ig_b), flip(lf_b), state_b0, with_output)
    if not with_output:
        return None, st_f, st_b
    hs = h_f + flip(h_b)
    B, H, N, dv = hs.shape
    hs = hs.transpose(0, 2, 1, 3).reshape(B, N, H * dv).astype(h.dtype)
    return (jax.nn.sigmoid(o) * hs) @ w_out, st_f, st_b


def _zero_state(batch):
    return (jnp.zeros((batch, MLSTM_HEADS, MLSTM_QK_DIM, MLSTM_V_DIM), jnp.float32),
            jnp.zeros((batch, MLSTM_HEADS, MLSTM_QK_DIM), jnp.float32),
            jnp.zeros((batch, MLSTM_HEADS), jnp.float32))


def _conv_ffn(h, w_in, conv_w, conv_b, w_out, grid):
    u, g = jnp.split(h @ w_in, 2, axis=-1)
    g = _dwconv_grid(g, conv_w, conv_b, grid)
    return (jax.nn.silu(g) * u) @ w_out


def setup_inputs(seed: int = 0) -> dict:
    key = jax.random.key(seed)
    ks = jax.random.split(key, 20)
    nrm = jax.random.normal
    f32 = jnp.float32
    b_gate = jnp.concatenate([0.1 * nrm(ks[10], (N_MLSTM_LAYERS, 2 * MLSTM_HEADS), f32),
                              3.0 + 0.5 * nrm(ks[11], (N_MLSTM_LAYERS, 2 * MLSTM_HEADS), f32)], axis=-1)
    return {
        'x': nrm(ks[0], (BATCH, SEQ, D_MODEL), f32),
        'c': nrm(ks[1], (BATCH, D_MODEL), f32),
        'ctx': nrm(ks[2], (BATCH, CTX_LEN, D_MODEL), f32),
        'c_ctx': nrm(ks[3], (D_MODEL,), f32),
        'mod_w': nrm(ks[4], (DEPTH, D_MODEL, N_MOD * D_MODEL), f32) * D_MODEL ** -0.5,
        'mod_b': 0.02 * nrm(ks[5], (DEPTH, N_MOD * D_MODEL), f32),
        'norm_g': 1.0 + 0.05 * nrm(ks[6], (DEPTH, 4, D_MODEL), f32),
        'pool_w': nrm(ks[7], (N_POOL_LAYERS, POOL_GROUPS, POOL_GROUP_DIM, POOL_GROUP_DIM), f32) * POOL_GROUP_DIM ** -0.5,
        'pool_scale': 1.0 + 0.05 * nrm(ks[8], (N_POOL_LAYERS, D_MODEL), f32),
        'mlstm_w_in': nrm(ks[9], (N_MLSTM_LAYERS, D_MODEL, MLSTM_IN_WIDTH), f32) * D_MODEL ** -0.5,
        'mlstm_b_gate': b_gate,
        'mlstm_conv_w': nrm(ks[12], (N_MLSTM_LAYERS, CONV_K, CONV_K, 1, 2 * MLSTM_QK_WIDTH), f32) / CONV_K,
        'mlstm_conv_b': 0.02 * nrm(ks[13], (N_MLSTM_LAYERS, 2 * MLSTM_QK_WIDTH), f32),
        'mlstm_w_out': nrm(ks[14], (N_MLSTM_LAYERS, MLSTM_V_WIDTH, D_MODEL), f32) * MLSTM_V_WIDTH ** -0.5,
        'ffn_w_in': nrm(ks[15], (DEPTH, D_MODEL, 2 * D_FF), f32) * D_MODEL ** -0.5,
        'ffn_conv_w': nrm(ks[16], (DEPTH, CONV_K, CONV_K, 1, D_FF), f32) / CONV_K,
        'ffn_conv_b': 0.02 * nrm(ks[17], (DEPTH, D_FF), f32),
        'ffn_w_out': nrm(ks[18], (DEPTH, D_FF, D_MODEL), f32) * D_FF ** -0.5,
    }


def reference(x, c, ctx, c_ctx, mod_w, mod_b, norm_g, pool_w, pool_scale, mlstm_w_in, mlstm_b_gate,
              mlstm_conv_w, mlstm_conv_b, mlstm_w_out, ffn_w_in, ffn_conv_w, ffn_conv_b, ffn_w_out):
    B, N, _ = x.shape
    ROWS = N // GRID_W
    grid_x = (ROWS, GRID_W)
    grid_c = (1, ctx.shape[1])
    for i in range(DEPTH):
        last = i == DEPTH - 1
        j = i // N_MIXERS
        use_pool = i % N_MIXERS == 0
        ctx_out = not last
        ctx_read = ctx_out or not use_pool
        sh1, sc1, gt1, sh2, sc2, gt2 = _modulation(c, mod_w[i], mod_b[i])
        hx = _modulate(x, norm_g[i, 0], sh1, sc1)
        if ctx_read:
            csh1, csc1, cgt1, csh2, csc2, cgt2 = _modulation(c_ctx[None, :], mod_w[i], mod_b[i])
            hc = _modulate(ctx, norm_g[i, 0], csh1, csc1)
        if use_pool:
            yx = _pool_mixer(hx, pool_w[j], pool_scale[j], grid_x)
            yc = _pool_mixer(hc, pool_w[j], pool_scale[j], grid_c) if ctx_out else None
        else:
            z0 = _zero_state(B)
            yc, st_f, st_b = _mlstm_mixer(hc, mlstm_w_in[j], mlstm_b_gate[j], mlstm_conv_w[j], mlstm_conv_b[j],
                                          mlstm_w_out[j], grid_c, z0, z0, ctx_out)
            yx, _, _ = _mlstm_mixer(hx, mlstm_w_in[j], mlstm_b_gate[j], mlstm_conv_w[j], mlstm_conv_b[j],
                                    mlstm_w_out[j], grid_x, st_f, st_b, True)
        x = x + gt1 * _rms_norm(yx, norm_g[i, 1])
        fx = _conv_ffn(_modulate(x, norm_g[i, 2], sh2, sc2), ffn_w_in[i], ffn_conv_w[i], ffn_conv_b[i], ffn_w_out[i], grid_x)
        x = x + gt2 * _rms_norm(fx, norm_g[i, 3])
        if ctx_out:
            ctx = ctx + cgt1 * _rms_norm(yc, norm_g[i, 1])
            fc = _conv_ffn(_modulate(ctx, norm_g[i, 2], csh2, csc2), ffn_w_in[i], ffn_conv_w[i], ffn_conv_b[i], ffn_w_out[i], grid_c)
            ctx = ctx + cgt2 * _rms_norm(fc, norm_g[i, 3])
    return x
```

```python
import functools

import numpy as np
import jax
import jax.numpy as jnp
from jax import lax
from jax.experimental import pallas as pl
from jax.experimental.pallas import tpu as pltpu

F32 = jnp.float32
BF16 = jnp.bfloat16

D_MODEL = 1024
GRID_W = 64
EPS = 1e-6
N_MOD = 6
POOL_WINDOWS = (2, 4, 8, 16)
POOL_GROUP_DIM = D_MODEL // len(POOL_WINDOWS)
HEADS = 8
V_DIM = D_MODEL // HEADS
QK_DIM = V_DIM // 2
QK_WIDTH = HEADS * QK_DIM
D_FF = ((8 * D_MODEL // 3 + 255) // 256) * 256
COND_ROWS = 16
LANES = 128
CHUNK = LANES
FF_COLS = 256
VMEM_LIMIT = 56 * 1024 * 1024


def _cparams(sem):
    return pltpu.CompilerParams(dimension_semantics=sem, vmem_limit_bytes=VMEM_LIMIT)


def _sigmoid(x):
    return 1.0 / (1.0 + jnp.exp(-x))


def _rms(xf, g):
    ms = jnp.mean(xf * xf, axis=-1, keepdims=True)
    return xf * lax.rsqrt(ms + EPS) * g


def _modulate(xf, g, shift, scale):
    return _rms(xf, g) * (1.0 + scale) + shift


def _dot(a, b):
    return jnp.dot(a, b, preferred_element_type=F32)


def _mod_kernel(c_ref, w_ref, b_ref, o_ref):
    c = c_ref[...]
    s = (c * _sigmoid(c)).astype(BF16)
    o_ref[...] = _dot(s, w_ref[...].astype(BF16)) + b_ref[...]


def _modulation(cond, mod_w, mod_b):
    depth, d, n = mod_w.shape
    tn = 1536
    return pl.pallas_call(
        _mod_kernel,
        grid=(depth, n // tn),
        in_specs=[pl.BlockSpec((COND_ROWS, d), lambda i, j: (0, 0)),
                  pl.BlockSpec((None, d, tn), lambda i, j: (i, 0, j)),
                  pl.BlockSpec((None, 1, tn), lambda i, j: (i, 0, j))],
        out_specs=pl.BlockSpec((None, COND_ROWS, tn), lambda i, j: (i, 0, j)),
        out_shape=jax.ShapeDtypeStruct((depth, COND_ROWS, n), F32),
        compiler_params=_cparams(("parallel", "parallel")),
        name="modulation",
    )(cond, mod_w, mod_b.reshape(depth, 1, n))


def _pool_kernel(x_ref, m_ref, ng_ref, pw_ref, ps_ref, o_ref, *, row_w):
    x = x_ref[...]
    m = m_ref[...]
    ng = ng_ref[...]
    h = _modulate(x, ng[0:1], m[0:1], m[1:2])
    tn = x.shape[0]
    gd = POOL_GROUP_DIM
    pos = lax.broadcasted_iota(jnp.int32, (tn, gd), 0) & (row_w - 1)

    def shifted(a, d):
        rolled = pltpu.roll(a, d % tn, axis=0)
        valid = (pos >= d) if d > 0 else (pos < row_w + d)
        return jnp.where(valid, rolled, 0.0)

    ys = []
    for gi, win in enumerate(POOL_WINDOWS):
        hg = h[:, gi * gd:(gi + 1) * gd]
        half = win // 2
        back, fwd, k = hg, hg, 1
        while k < half:
            back = back + shifted(back, k)
            fwd = fwd + shifted(fwd, -k)
            k *= 2
        total = shifted(back, 1) + fwd
        cnt = jnp.minimum(pos + half, row_w) - jnp.maximum(pos - half, 0)
        y = total * (1.0 / cnt.astype(F32)) - hg
        ys.append(_dot(y.astype(BF16), pw_ref[gi]))
    y = jnp.concatenate(ys, axis=1) * ps_ref[...]
    o_ref[...] = x + m[2:3] * _rms(y, ng[1:2])


def _pool_layer(x, mods, ng, pool_w, pool_scale, row_w, tn):
    b, n, d = x.shape
    per_batch = mods.shape[0] > 1
    return pl.pallas_call(
        functools.partial(_pool_kernel, row_w=row_w),
        grid=(b, n // tn),
        in_specs=[pl.BlockSpec((None, tn, d), lambda i, t: (i, t, 0)),
                  pl.BlockSpec((None, N_MOD, d), (lambda i, t: (i, 0, 0)) if per_batch else (lambda i, t: (0, 0, 0))),
                  pl.BlockSpec((4, d), lambda i, t: (0, 0)),
                  pl.BlockSpec(pool_w.shape, lambda i, t: (0, 0, 0)),
                  pl.BlockSpec((1, d), lambda i, t: (0, 0))],
        out_specs=pl.BlockSpec((None, tn, d), lambda i, t: (i, t, 0)),
        out_shape=jax.ShapeDtypeStruct(x.shape, F32),
        compiler_params=_cparams(("parallel", "parallel")),
        name="pool_mixer",
    )(x, mods, ng, pool_w, pool_scale)


def _conv3x3(g, w9, bias, row_w, tm, vertical):
    rows = g.shape[0]
    pos = lax.broadcasted_iota(jnp.int32, g.shape, 0) & (row_w - 1)
    left = jnp.where(pos >= 1, pltpu.roll(g, 1, axis=0), 0.0)
    right = jnp.where(pos < row_w - 1, pltpu.roll(g, rows - 1, axis=0), 0.0)
    acc = None
    for dr in ((-1, 0, 1) if vertical else (0,)):
        off = (row_w if vertical else 0) + dr * row_w
        for dc, arr in ((-1, left), (0, g), (1, right)):
            tap = (dr + 1) * 3 + (dc + 1)
            term = w9[tap:tap + 1] * arr[off:off + tm]
            acc = term if acc is None else acc + term
    return acc + bias


def _fill_modulated(hh_ref, x, halo_refs, ng_row, shift, scale, row_w, tm):
    if halo_refs is None:
        hh_ref[...] = _modulate(x, ng_row, shift, scale).astype(BF16)
        return
    top_ref, bot_ref = halo_refs
    t = pl.program_id(1)
    last = pl.num_programs(1) - 1
    top = _modulate(top_ref[...], ng_row, shift, scale) * (t > 0).astype(F32)
    bot = _modulate(bot_ref[...], ng_row, shift, scale) * (t < last).astype(F32)
    hh_ref[0:row_w] = top.astype(BF16)
    hh_ref[row_w:row_w + tm] = _modulate(x, ng_row, shift, scale).astype(BF16)
    hh_ref[row_w + tm:row_w + tm + row_w] = bot.astype(BF16)


def _tile_specs(d, tm, row_w, vertical, n):
    main = pl.BlockSpec((None, tm, d), lambda i, t: (i, t, 0))
    if not vertical:
        return [main]
    r = tm // row_w
    n_rows = n // row_w
    top = pl.BlockSpec((None, row_w, d), lambda i, t: (i, jnp.maximum(t * r - 1, 0), 0))
    bot = pl.BlockSpec((None, row_w, d), lambda i, t: (i, jnp.minimum((t + 1) * r, n_rows - 1), 0))
    return [main, top, bot]


def _mod_spec(mods, d):
    if mods.shape[0] > 1:
        return pl.BlockSpec((None, N_MOD, d), lambda i, t: (i, 0, 0))
    return pl.BlockSpec((None, N_MOD, d), lambda i, t: (0, 0, 0))


def _const_spec(a):
    nd = a.ndim
    return pl.BlockSpec(a.shape, lambda i, t: (0,) * nd, pipeline_mode=pl.Buffered(1))


def _ffn_kernel(*refs, row_w, tm, vertical):
    if vertical:
        x_ref, top_ref, bot_ref = refs[:3]
        halo = (top_ref, bot_ref)
        rest = refs[3:]
    else:
        x_ref = refs[0]
        halo = None
        rest = refs[1:]
    m_ref, ng_ref, wg_ref, wu_ref, cw_ref, cb_ref, wo_ref, o_ref, hh_ref, a_ref = rest
    x = x_ref[...]
    m = m_ref[...]
    ng = ng_ref[...]
    _fill_modulated(hh_ref, x, halo, ng[2:3], m[3:4], m[4:5], row_w, tm)
    lo = row_w if vertical else 0
    for j in range(D_FF // FF_COLS):
        cs = slice(j * FF_COLS, (j + 1) * FF_COLS)
        g = _dot(hh_ref[...], wg_ref[:, cs])
        cv = _conv3x3(g, cw_ref[:, cs], cb_ref[:, cs], row_w, tm, vertical)
        u = _dot(hh_ref[lo:lo + tm], wu_ref[:, cs])
        a_ref[:, cs] = (cv * _sigmoid(cv) * u).astype(BF16)
    f = _dot(a_ref[...], wo_ref[...])
    o_ref[...] = x + m[5:6] * _rms(f, ng[3:4])


def _ffn_layer(x, mods, ng, wg, wu, cw, cb, wo, row_w, tm, vertical):
    b, n, d = x.shape
    tmh = tm + 2 * row_w if vertical else tm
    x_specs = _tile_specs(d, tm, row_w, vertical, n)
    consts = (ng, wg, wu, cw, cb, wo)
    return pl.pallas_call(
        functools.partial(_ffn_kernel, row_w=row_w, tm=tm, vertical=vertical),
        grid=(b, n // tm),
        in_specs=x_specs + [_mod_spec(mods, d)] + [_const_spec(a) for a in consts],
        out_specs=pl.BlockSpec((None, tm, d), lambda i, t: (i, t, 0)),
        out_shape=jax.ShapeDtypeStruct(x.shape, F32),
        scratch_shapes=[pltpu.VMEM((tmh, d), BF16), pltpu.VMEM((tm, D_FF), BF16)],
        compiler_params=_cparams(("parallel", "parallel")),
        name="conv_ffn",
    )(*([x] * len(x_specs)), mods, *consts)


def _proj_kernel(*refs, row_w, tm, vertical):
    if vertical:
        x_ref, top_ref, bot_ref = refs[:3]
        halo = (top_ref, bot_ref)
        rest = refs[3:]
    else:
        x_ref = refs[0]
        halo = None
        rest = refs[1:]
    (m_ref, ng_ref, wqk_ref, wv_ref, wo_ref, wgt_ref, cw_ref, cb_ref, ks_ref, bg_ref,
     qk_ref, v_ref, og_ref, gt_ref, hh_ref) = rest
    m = m_ref[...]
    ng = ng_ref[...]
    _fill_modulated(hh_ref, x_ref[...], halo, ng[0:1], m[0:1], m[1:2], row_w, tm)
    lo = row_w if vertical else 0
    for j in range(2 * QK_WIDTH // FF_COLS):
        cs = slice(j * FF_COLS, (j + 1) * FF_COLS)
        g = _dot(hh_ref[...], wqk_ref[:, cs])
        cv = _conv3x3(g, cw_ref[:, cs], cb_ref[:, cs], row_w, tm, vertical)
        qk_ref[:, cs] = (cv * _sigmoid(cv) * ks_ref[:, cs]).astype(BF16)
    h = hh_ref[lo:lo + tm]
    v_ref[...] = _dot(h, wv_ref[...]).astype(BF16)
    og_ref[...] = _sigmoid(_dot(h, wo_ref[...])).astype(BF16)
    gt_ref[...] = _dot(h, wgt_ref[...]) + bg_ref[...]


def _proj_layer(x, mods, ng, wqk, wv, wo, wgt, cw, cb, kscale, bgate, row_w, tm, vertical):
    b, n, d = x.shape
    tmh = tm + 2 * row_w if vertical else tm
    x_specs = _tile_specs(d, tm, row_w, vertical, n)
    consts = (ng, wqk, wv, wo, wgt, cw, cb, kscale, bgate)
    tile = lambda w: pl.BlockSpec((None, tm, w), lambda i, t: (i, t, 0))
    return pl.pallas_call(
        functools.partial(_proj_kernel, row_w=row_w, tm=tm, vertical=vertical),
        grid=(b, n // tm),
        in_specs=x_specs + [_mod_spec(mods, d)] + [_const_spec(a) for a in consts],
        out_specs=[tile(d), tile(d), tile(d), tile(LANES)],
        out_shape=[jax.ShapeDtypeStruct((b, n, d), BF16), jax.ShapeDtypeStruct((b, n, d), BF16),
                   jax.ShapeDtypeStruct((b, n, d), BF16), jax.ShapeDtypeStruct((b, n, LANES), F32)],
        scratch_shapes=[pltpu.VMEM((tmh, d), BF16)],
        compiler_params=_cparams(("parallel", "parallel")),
        name="mlstm_proj",
    )(*([x] * len(x_specs)), mods, *consts)


def _lane_scan(x, op, ident, reverse):
    lane = lax.broadcasted_iota(jnp.int32, x.shape, 1)
    d = 1
    while d < LANES:
        if reverse:
            moved = jnp.where(lane < LANES - d, pltpu.roll(x, LANES - d, axis=1), ident)
        else:
            moved = jnp.where(lane >= d, pltpu.roll(x, d, axis=1), ident)
        x = op(x, moved)
        d *= 2
    return x


def _log_sigmoid(x):
    return jnp.minimum(x, 0.0) - jnp.log(1.0 + jnp.exp(-jnp.abs(x)))


_COL_ALPHA, _COL_AT, _COL_EM, _COL_WEND = 0, 1, 2, 3
_ROW_BETA, _ROW_A, _ROW_E = 0, 1, 2


def _core_kernel(*refs, seq, with_output):
    nc = seq // CHUNK
    if with_output:
        qk_ref, v_ref, og_ref, gt_ref, cn0_ref, m0_ref, hs_ref, col_ref, row_ref, cn_ref, hacc_ref = refs
    else:
        qk_ref, v_ref, gt_ref, cn0_ref, m0_ref, cn_out_ref, m_out_ref, col_ref, row_ref, cn_ref = refs

    parts = [[], [], [], []]
    for c in range(nc):
        blk = gt_ref[c * CHUNK:(c + 1) * CHUNK, :].T
        for q in range(4):
            parts[q].append(blk[q * HEADS:(q + 1) * HEADS])
    ig = [jnp.concatenate(parts[0], axis=0), jnp.concatenate(parts[1], axis=0)]
    lf = [_log_sigmoid(jnp.concatenate(parts[2], axis=0)), _log_sigmoid(jnp.concatenate(parts[3], axis=0))]
    rows = nc * HEADS
    full = (rows, CHUNK)

    for dr in range(2):
        rev = dr == 1
        b = _lane_scan(lf[dr], jnp.add, 0.0, rev)
        edge = 0 if rev else CHUNK - 1
        bl = jnp.broadcast_to(b[:, edge:edge + 1], full)
        g = bl - b + ig[dr]
        gm = jnp.broadcast_to(jnp.max(g, axis=1, keepdims=True), full)
        wend = jnp.exp(g - gm)
        beta = ig[dr] - b
        cm = _lane_scan(beta, jnp.maximum, -jnp.inf, rev)
        m = m0_ref[dr]
        mprev, aa, ee = [None] * nc, [None] * nc, [None] * nc
        for c in (reversed(range(nc)) if rev else range(nc)):
            sl = slice(c * HEADS, (c + 1) * HEADS)
            m_new = jnp.maximum(bl[sl] + m, gm[sl])
            mprev[c] = m
            aa[c] = jnp.exp(bl[sl] + m - m_new)
            ee[c] = jnp.exp(gm[sl] - m_new)
            m = m_new
        if not with_output:
            m_out_ref[dr] = m
        mprev = jnp.concatenate(mprev, axis=0)
        alpha = -jnp.maximum(mprev, cm)
        at = jnp.exp(mprev + alpha)
        em = jnp.exp(alpha - b)
        row_ref[dr, _ROW_BETA] = beta.reshape(nc, HEADS, CHUNK)
        row_ref[dr, _ROW_A] = jnp.concatenate(aa, axis=0).reshape(nc, HEADS, CHUNK)
        row_ref[dr, _ROW_E] = jnp.concatenate(ee, axis=0).reshape(nc, HEADS, CHUNK)
        pad = jnp.zeros((LANES - 4 * HEADS, CHUNK), F32)
        for c in range(nc):
            sl = slice(c * HEADS, (c + 1) * HEADS)
            stacked = jnp.concatenate([alpha[sl], at[sl], em[sl], wend[sl], pad], axis=0)
            col_ref[dr, c] = stacked.T

    cn_ref[...] = cn0_ref[...]
    ti = lax.broadcasted_iota(jnp.int32, (CHUNK, CHUNK), 0)
    si = lax.broadcasted_iota(jnp.int32, (CHUNK, CHUNK), 1)
    ones_col = (lax.broadcasted_iota(jnp.int32, (CHUNK, LANES), 1) == 0).astype(BF16)

    def chunk_step(c, dr):
        r0 = pl.multiple_of(c * CHUNK, CHUNK)
        for h in range(HEADS):
            hs_ = slice(h * V_DIM, (h + 1) * V_DIM)
            qkh = qk_ref[pl.ds(r0, CHUNK), hs_]
            q = qkh[:, :QK_DIM]
            k = qkh[:, QK_DIM:]
            vh = v_ref[pl.ds(r0, CHUNK), hs_]
            cn = cn_ref[dr, h]

            def col(qi):
                j = qi * HEADS + h
                return col_ref[dr, c, :, j:j + 1]

            if with_output:
                s = lax.dot_general(q, k, (((1,), (1,)), ((), ())), preferred_element_type=F32)
                beta = row_ref[dr, _ROW_BETA, c, h:h + 1, :]
                mask = (si <= ti) if dr == 0 else (si >= ti)
                p = s * jnp.where(mask, jnp.exp(col(_COL_ALPHA) + beta), 0.0)
                qc = _dot(q, cn.astype(BF16))
                at = col(_COL_AT)
                num = _dot(p.astype(BF16), vh) + at * qc[:, :V_DIM]
                den = jnp.sum(p, axis=1, keepdims=True) + at * qc[:, V_DIM:V_DIM + 1]
                hout = num / jnp.maximum(jnp.abs(den), col(_COL_EM))
                if dr == 0:
                    hacc_ref[pl.ds(r0, CHUNK), hs_] = hout
                else:
                    tot = hacc_ref[pl.ds(r0, CHUNK), hs_] + hout
                    hs_ref[pl.ds(r0, CHUNK), hs_] = (og_ref[pl.ds(r0, CHUNK), hs_].astype(F32) * tot).astype(BF16)

            kw = (k.astype(F32) * col(_COL_WEND)).astype(BF16)
            vaug = jnp.concatenate([vh, ones_col], axis=1)
            cc = lax.dot_general(kw, vaug, (((0,), (0,)), ((), ())), preferred_element_type=F32)
            a = row_ref[dr, _ROW_A, c, h:h + 1, :]
            e = row_ref[dr, _ROW_E, c, h:h + 1, :]
            a2 = jnp.concatenate([a, a], axis=1)
            e2 = jnp.concatenate([e, e], axis=1)
            cn_ref[dr, h] = a2 * cn + e2 * cc

    def fwd_body(i, carry):
        chunk_step(i, 0)
        return carry

    def bwd_body(i, carry):
        chunk_step(nc - 1 - i, 1)
        return carry

    lax.fori_loop(0, nc, fwd_body, 0)
    lax.fori_loop(0, nc, bwd_body, 0)
    if not with_output:
        cn_out_ref[...] = cn_ref[...]


def _core_layer(qk, v, og, gt, cn0, m0, with_output):
    b, seq, d = qk.shape
    nc = seq // CHUNK
    tok = lambda w: pl.BlockSpec((None, seq, w), lambda i: (i, 0, 0))
    cn_spec = pl.BlockSpec((None, 2, HEADS, QK_DIM, 2 * LANES), lambda i: (i, 0, 0, 0, 0))
    m_spec = pl.BlockSpec((None, 2, HEADS, LANES), lambda i: (i, 0, 0, 0))
    scratch = [pltpu.VMEM((2, nc, CHUNK, LANES), F32),
               pltpu.VMEM((2, 3, nc, HEADS, CHUNK), F32),
               pltpu.VMEM((2, HEADS, QK_DIM, 2 * LANES), F32)]
    if with_output:
        ins = (qk, v, og, gt, cn0, m0)
        in_specs = [tok(d), tok(d), tok(d), tok(LANES), cn_spec, m_spec]
        out_specs = tok(d)
        out_shape = jax.ShapeDtypeStruct((b, seq, d), BF16)
        scratch = scratch + [pltpu.VMEM((seq, d), F32)]
    else:
        ins = (qk, v, gt, cn0, m0)
        in_specs = [tok(d), tok(d), tok(LANES), cn_spec, m_spec]
        out_specs = [cn_spec, m_spec]
        out_shape = [jax.ShapeDtypeStruct(cn0.shape, F32), jax.ShapeDtypeStruct(m0.shape, F32)]
    return pl.pallas_call(
        functools.partial(_core_kernel, seq=seq, with_output=with_output),
        grid=(b,),
        in_specs=in_specs,
        out_specs=out_specs,
        out_shape=out_shape,
        scratch_shapes=scratch,
        compiler_params=_cparams(("parallel",)),
        name="mlstm_core",
    )(*ins)


def _outproj_kernel(x_ref, hs_ref, m_ref, ng_ref, w_ref, o_ref):
    y = _dot(hs_ref[...], w_ref[...])
    o_ref[...] = x_ref[...] + m_ref[...][2:3] * _rms(y, ng_ref[...][1:2])


def _outproj_layer(x, hs, mods, ng, w_out, tm):
    b, n, d = x.shape
    tile = pl.BlockSpec((None, tm, d), lambda i, t: (i, t, 0))
    return pl.pallas_call(
        _outproj_kernel,
        grid=(b, n // tm),
        in_specs=[tile, tile, _mod_spec(mods, d), _const_spec(ng), _const_spec(w_out)],
        out_specs=tile,
        out_shape=jax.ShapeDtypeStruct(x.shape, F32),
        compiler_params=_cparams(("parallel", "parallel")),
        name="mlstm_outproj",
    )(x, hs, mods, ng, w_out)


def _head_major_perm():
    p = np.arange(2 * QK_WIDTH)
    h, r = p // V_DIM, p % V_DIM
    return np.where(r < QK_DIM, h * QK_DIM + r, QK_WIDTH + h * QK_DIM + (r - QK_DIM))


def kernel(x, c, ctx, c_ctx, mod_w, mod_b, norm_g, pool_w, pool_scale, mlstm_w_in, mlstm_b_gate, mlstm_conv_w,
           mlstm_conv_b, mlstm_w_out, ffn_w_in, ffn_conv_w, ffn_conv_b, ffn_w_out):
    b, n, d = x.shape
    ctx_len = ctx.shape[1]
    cond = jnp.zeros((COND_ROWS, d), F32).at[:b].set(c).at[b].set(c_ctx)
    mods = _modulation(cond, mod_w, mod_b).reshape(mod_w.shape[0], COND_ROWS, N_MOD, d)
    ctx_flat = ctx.reshape(1, b * ctx_len, d)

    def ffn_weights(i):
        w_in = ffn_w_in[i].astype(BF16)
        return (w_in[:, D_FF:], w_in[:, :D_FF], ffn_conv_w[i].reshape(9, D_FF), ffn_conv_b[i].reshape(1, D_FF),
                ffn_w_out[i].astype(BF16))

    mx, mc = mods[0, :b], mods[0, b:b + 1]
    pw = pool_w[0].astype(BF16)
    ps = pool_scale[0].reshape(1, d)
    x = _pool_layer(x, mx, norm_g[0], pw, ps, GRID_W, 512)
    ctx_flat = _pool_layer(ctx_flat, mc, norm_g[0], pw, ps, ctx_len, 512)
    fw = ffn_weights(0)
    x = _ffn_layer(x, mx, norm_g[0], *fw, row_w=GRID_W, tm=1024, vertical=True)
    ctx_flat = _ffn_layer(ctx_flat, mc, norm_g[0], *fw, row_w=ctx_len, tm=1024, vertical=False)

    mx, mc = mods[1, :b], mods[1, b:b + 1]
    perm = _head_major_perm()
    w_in = mlstm_w_in[0]
    wqk = w_in[:, :2 * QK_WIDTH][:, perm].astype(BF16)
    wv = w_in[:, 2 * QK_WIDTH:2 * QK_WIDTH + d].astype(BF16)
    wo = w_in[:, 2 * QK_WIDTH + d:2 * QK_WIDTH + 2 * d].astype(BF16)
    wgt = jnp.zeros((d, LANES), F32).at[:, :4 * HEADS].set(w_in[:, 2 * QK_WIDTH + 2 * d:]).astype(BF16)
    bgate = jnp.zeros((1, LANES), F32).at[0, :4 * HEADS].set(mlstm_b_gate[0])
    cw = mlstm_conv_w[0].reshape(9, 2 * QK_WIDTH)[:, perm]
    cb = mlstm_conv_b[0][perm].reshape(1, 2 * QK_WIDTH)
    kscale = jnp.asarray(np.where(np.arange(2 * QK_WIDTH) % V_DIM < QK_DIM, 1.0, QK_DIM ** -0.5), F32).reshape(1, -1)
    pargs = (norm_g[1], wqk, wv, wo, wgt, cw, cb, kscale, bgate)

    qk_c, v_c, _, gt_c = _proj_layer(ctx_flat, mc, *pargs, row_w=ctx_len, tm=1024, vertical=False)
    per_batch = lambda a: a.reshape(b, ctx_len, a.shape[-1])
    cn0 = jnp.zeros((b, 2, HEADS, QK_DIM, 2 * LANES), F32)
    m0 = jnp.zeros((b, 2, HEADS, LANES), F32)
    cn1, m1 = _core_layer(per_batch(qk_c), per_batch(v_c), None, per_batch(gt_c), cn0, m0, with_output=False)

    qk, v, og, gt = _proj_layer(x, mx, *pargs, row_w=GRID_W, tm=1024, vertical=True)
    hs = _core_layer(qk, v, og, gt, cn1, m1, with_output=True)
    x = _outproj_layer(x, hs, mx, norm_g[1], mlstm_w_out[0].astype(BF16), 1024)
    x = _ffn_layer(x, mx, norm_g[1], *ffn_weights(1), row_w=GRID_W, tm=1024, vertical=True)
    return x
```

```python
import functools

import jax
import jax.numpy as jnp
from jax import lax
from jax.experimental import pallas as pl
from jax.experimental.pallas import tpu as pltpu

F32 = jnp.float32
BF16 = jnp.bfloat16

D_MODEL = 1024
GRID_W = 64
EPS = 1e-6
N_MOD = 6
POOL_WINDOWS = (2, 4, 8, 16)
POOL_GROUP_DIM = D_MODEL // len(POOL_WINDOWS)
HEADS = 8
V_DIM = D_MODEL // HEADS
QK_DIM = V_DIM // 2
QK_WIDTH = HEADS * QK_DIM
D_FF = ((8 * D_MODEL // 3 + 255) // 256) * 256
COND_ROWS = 16
LANES = 128
CHUNK = LANES
FF_COLS = 256
VMEM_LIMIT = 56 * 1024 * 1024


def _cparams(sem):
    return pltpu.CompilerParams(dimension_semantics=sem, vmem_limit_bytes=VMEM_LIMIT)


def _sigmoid(x):
    return 1.0 / (1.0 + jnp.exp(-x))


def _rms(xf, g):
    ms = jnp.mean(xf * xf, axis=-1, keepdims=True)
    return xf * lax.rsqrt(ms + EPS) * g


def _modulate(xf, g, shift, scale):
    return _rms(xf, g) * (1.0 + scale) + shift


def _dot(a, b):
    return jnp.dot(a, b, preferred_element_type=F32)


def _mod_kernel(c_ref, w_ref, b_ref, o_ref):
    c = c_ref[...]
    s = (c * _sigmoid(c)).astype(BF16)
    o_ref[...] = _dot(s, w_ref[...].astype(BF16)) + b_ref[...]


def _modulation(cond, mod_w, mod_b):
    depth, d, n = mod_w.shape
    tn = 1536
    return pl.pallas_call(
        _mod_kernel,
        grid=(depth, n // tn),
        in_specs=[pl.BlockSpec((COND_ROWS, d), lambda i, j: (0, 0)),
                  pl.BlockSpec((None, d, tn), lambda i, j: (i, 0, j)),
                  pl.BlockSpec((None, 1, tn), lambda i, j: (i, 0, j))],
        out_specs=pl.BlockSpec((None, COND_ROWS, tn), lambda i, j: (i, 0, j)),
        out_shape=jax.ShapeDtypeStruct((depth, COND_ROWS, n), F32),
        compiler_params=_cparams(("parallel", "parallel")),
        name="modulation",
    )(cond, mod_w, mod_b.reshape(depth, 1, n))


def _pool_kernel(x_ref, m_ref, ng_ref, pw_ref, ps_ref, o_ref, *, row_w):
    x = x_ref[...]
    m = m_ref[...]
    ng = ng_ref[...]
    h = _modulate(x, ng[0:1], m[0:1], m[1:2])
    tn = x.shape[0]
    gd = POOL_GROUP_DIM
    pos = lax.broadcasted_iota(jnp.int32, (tn, gd), 0) & (row_w - 1)

    def shifted(a, d):
        rolled = pltpu.roll(a, d % tn, axis=0)
        valid = (pos >= d) if d > 0 else (pos < row_w + d)
        return jnp.where(valid, rolled, 0.0)

    ys = []
    for gi, win in enumerate(POOL_WINDOWS):
        hg = h[:, gi * gd:(gi + 1) * gd]
        half = win // 2
        back, fwd, k = hg, hg, 1
        while k < half:
            back = back + shifted(back, k)
            fwd = fwd + shifted(fwd, -k)
            k *= 2
        total = shifted(back, 1) + fwd
        cnt = jnp.minimum(pos + half, row_w) - jnp.maximum(pos - half, 0)
        y = total * (1.0 / cnt.astype(F32)) - hg
        ys.append(_dot(y.astype(BF16), pw_ref[gi]))
    y = jnp.concatenate(ys, axis=1) * ps_ref[...]
    o_ref[...] = x + m[2:3] * _rms(y, ng[1:2])


def _pool_layer(x, mods, ng, pool_w, pool_scale, row_w, tn):
    b, n, d = x.shape
    per_batch = mods.shape[0] > 1
    return pl.pallas_call(
        functools.partial(_pool_kernel, row_w=row_w),
        grid=(b, n // tn),
        in_specs=[pl.BlockSpec((None, tn, d), lambda i, t: (i, t, 0)),
                  pl.BlockSpec((None, N_MOD, d), (lambda i, t: (i, 0, 0)) if per_batch else (lambda i, t: (0, 0, 0))),
                  pl.BlockSpec((4, d), lambda i, t: (0, 0)),
                  pl.BlockSpec(pool_w.shape, lambda i, t: (0, 0, 0)),
                  pl.BlockSpec((1, d), lambda i, t: (0, 0))],
        out_specs=pl.BlockSpec((None, tn, d), lambda i, t: (i, t, 0)),
        out_shape=jax.ShapeDtypeStruct(x.shape, F32),
        compiler_params=_cparams(("parallel", "parallel")),
        name="pool_mixer",
    )(x, mods, ng, pool_w, pool_scale)


def _conv3x3(g, w9, bias, row_w, tm, vertical):
    rows = g.shape[0]
    pos = lax.broadcasted_iota(jnp.int32, g.shape, 0) & (row_w - 1)
    left = jnp.where(pos >= 1, pltpu.roll(g, 1, axis=0), 0.0)
    right = jnp.where(pos < row_w - 1, pltpu.roll(g, rows - 1, axis=0), 0.0)
    acc = None
    for dr in ((-1, 0, 1) if vertical else (0,)):
        off = (row_w if vertical else 0) + dr * row_w
        for dc, arr in ((-1, left), (0, g), (1, right)):
            tap = (dr + 1) * 3 + (dc + 1)
            term = w9[tap:tap + 1] * arr[off:off + tm]
            acc = term if acc is None else acc + term
    return acc + bias


def _fill_modulated(hh_ref, x, halo_refs, ng_row, shift, scale, row_w, tm):
    if halo_refs is None:
        hh_ref[...] = _modulate(x, ng_row, shift, scale).astype(BF16)
        return
    top_ref, bot_ref = halo_refs
    t = pl.program_id(1)
    last = pl.num_programs(1) - 1
    top = _modulate(top_ref[...], ng_row, shift, scale) * (t > 0).astype(F32)
    bot = _modulate(bot_ref[...], ng_row, shift, scale) * (t < last).astype(F32)
    hh_ref[0:row_w] = top.astype(BF16)
    hh_ref[row_w:row_w + tm] = _modulate(x, ng_row, shift, scale).astype(BF16)
    hh_ref[row_w + tm:row_w + tm + row_w] = bot.astype(BF16)


def _tile_specs(d, tm, row_w, vertical, n):
    main = pl.BlockSpec((None, tm, d), lambda i, t: (i, t, 0))
    if not vertical:
        return [main]
    r = tm // row_w
    n_rows = n // row_w
    top = pl.BlockSpec((None, row_w, d), lambda i, t: (i, jnp.maximum(t * r - 1, 0), 0))
    bot = pl.BlockSpec((None, row_w, d), lambda i, t: (i, jnp.minimum((t + 1) * r, n_rows - 1), 0))
    return [main, top, bot]


def _mod_spec(mods, d):
    if mods.shape[0] > 1:
        return pl.BlockSpec((None, N_MOD, d), lambda i, t: (i, 0, 0))
    return pl.BlockSpec((None, N_MOD, d), lambda i, t: (0, 0, 0))


def _const_spec(a):
    nd = a.ndim
    return pl.BlockSpec(a.shape, lambda i, t: (0,) * nd, pipeline_mode=pl.Buffered(1))


def _ffn_kernel(*refs, row_w, tm, vertical):
    if vertical:
        x_ref, top_ref, bot_ref = refs[:3]
        halo = (top_ref, bot_ref)
        rest = refs[3:]
    else:
        x_ref = refs[0]
        halo = None
        rest = refs[1:]
    m_ref, ng_ref, wg_ref, wu_ref, cw_ref, cb_ref, wo_ref, o_ref, hh_ref, a_ref = rest
    x = x_ref[...]
    m = m_ref[...]
    ng = ng_ref[...]
    _fill_modulated(hh_ref, x, halo, ng[2:3], m[3:4], m[4:5], row_w, tm)
    lo = row_w if vertical else 0
    for j in range(D_FF // FF_COLS):
        cs = slice(j * FF_COLS, (j + 1) * FF_COLS)
        g = _dot(hh_ref[...], wg_ref[:, cs])
        cv = _conv3x3(g, cw_ref[:, cs], cb_ref[:, cs], row_w, tm, vertical)
        u = _dot(hh_ref[lo:lo + tm], wu_ref[:, cs])
        a_ref[:, cs] = (cv * _sigmoid(cv) * u).astype(BF16)
    f = _dot(a_ref[...], wo_ref[...])
    o_ref[...] = x + m[5:6] * _rms(f, ng[3:4])


def _ffn_layer(x, mods, ng, wg, wu, cw, cb, wo, row_w, tm, vertical):
    b, n, d = x.shape
    tmh = tm + 2 * row_w if vertical else tm
    x_specs = _tile_specs(d, tm, row_w, vertical, n)
    consts = (ng, wg, wu, cw, cb, wo)
    return pl.pallas_call(
        functools.partial(_ffn_kernel, row_w=row_w, tm=tm, vertical=vertical),
        grid=(b, n // tm),
        in_specs=x_specs + [_mod_spec(mods, d)] + [_const_spec(a) for a in consts],
        out_specs=pl.BlockSpec((None, tm, d), lambda i, t: (i, t, 0)),
        out_shape=jax.ShapeDtypeStruct(x.shape, F32),
        scratch_shapes=[pltpu.VMEM((tmh, d), BF16), pltpu.VMEM((tm, D_FF), BF16)],
        compiler_params=_cparams(("parallel", "parallel")),
        name="conv_ffn",
    )(*([x] * len(x_specs)), mods, *consts)


def _proj_kernel(*refs, row_w, tm, vertical):
    if vertical:
        x_ref, top_ref, bot_ref = refs[:3]
        halo = (top_ref, bot_ref)
        rest = refs[3:]
    else:
        x_ref = refs[0]
        halo = None
        rest = refs[1:]
    (m_ref, ng_ref, wqk_ref, wvt_ref, wo_ref, wgt_ref, cw_ref, cb_ref, bg_ref,
     qt_ref, k_ref, vt_ref, og_ref, gt_ref, hh_ref) = rest
    m = m_ref[...]
    ng = ng_ref[...]
    _fill_modulated(hh_ref, x_ref[...], halo, ng[0:1], m[0:1], m[1:2], row_w, tm)
    lo = row_w if vertical else 0
    n_chunks = tm // CHUNK
    for j in range(2 * QK_WIDTH // FF_COLS):
        cs = slice(j * FF_COLS, (j + 1) * FF_COLS)
        g = _dot(hh_ref[...], wqk_ref[:, cs])
        cv = _conv3x3(g, cw_ref[:, cs], cb_ref[:, cs], row_w, tm, vertical)
        act = cv * _sigmoid(cv)
        if j < QK_WIDTH // FF_COLS:
            for ci in range(n_chunks):
                qt_ref[ci, cs, :] = act[ci * CHUNK:(ci + 1) * CHUNK].T.astype(BF16)
        else:
            k_ref[:, j * FF_COLS - QK_WIDTH:(j + 1) * FF_COLS - QK_WIDTH] = (act * QK_DIM ** -0.5).astype(BF16)
    h = hh_ref[lo:lo + tm]
    vt = lax.dot_general(wvt_ref[...], h, (((1,), (1,)), ((), ())), preferred_element_type=F32)
    for ci in range(n_chunks):
        vt_ref[ci] = vt[:, ci * CHUNK:(ci + 1) * CHUNK].astype(BF16)
    og_ref[...] = _sigmoid(_dot(h, wo_ref[...])).astype(BF16)
    gt_ref[...] = _dot(h, wgt_ref[...]) + bg_ref[...]


def _proj_layer(x, mods, ng, wqk, wvt, wo, wgt, cw, cb, bgate, row_w, tm, vertical):
    b, n, d = x.shape
    tmh = tm + 2 * row_w if vertical else tm
    x_specs = _tile_specs(d, tm, row_w, vertical, n)
    consts = (ng, wqk, wvt, wo, wgt, cw, cb, bgate)
    tile = lambda w: pl.BlockSpec((None, tm, w), lambda i, t: (i, t, 0))
    slab = lambda r: pl.BlockSpec((None, tm // CHUNK, r, CHUNK), lambda i, t: (i, t, 0, 0))
    nc = n // CHUNK
    return pl.pallas_call(
        functools.partial(_proj_kernel, row_w=row_w, tm=tm, vertical=vertical),
        grid=(b, n // tm),
        in_specs=x_specs + [_mod_spec(mods, d)] + [_const_spec(a) for a in consts],
        out_specs=[slab(QK_WIDTH), tile(QK_WIDTH), slab(d), tile(d), tile(LANES)],
        out_shape=[jax.ShapeDtypeStruct((b, nc, QK_WIDTH, CHUNK), BF16), jax.ShapeDtypeStruct((b, n, QK_WIDTH), BF16),
                   jax.ShapeDtypeStruct((b, nc, d, CHUNK), BF16), jax.ShapeDtypeStruct((b, n, d), BF16),
                   jax.ShapeDtypeStruct((b, n, LANES), F32)],
        scratch_shapes=[pltpu.VMEM((tmh, d), BF16)],
        compiler_params=_cparams(("parallel", "parallel")),
        name="mlstm_proj",
    )(*([x] * len(x_specs)), mods, *consts)


def _lane_scan(x, op, ident, reverse):
    lane = lax.broadcasted_iota(jnp.int32, x.shape, 1)
    d = 1
    while d < LANES:
        if reverse:
            moved = jnp.where(lane < LANES - d, pltpu.roll(x, LANES - d, axis=1), ident)
        else:
            moved = jnp.where(lane >= d, pltpu.roll(x, d, axis=1), ident)
        x = op(x, moved)
        d *= 2
    return x


def _log_sigmoid(x):
    return jnp.minimum(x, 0.0) - jnp.log(1.0 + jnp.exp(-jnp.abs(x)))


_ROW_ALPHA, _ROW_AT, _ROW_EM, _ROW_WEND, _ROW_A, _ROW_E = range(6)
STATE_ROWS = V_DIM + 16


def _core_kernel(*refs, seq, with_output):
    nc = seq // CHUNK
    if with_output:
        qt_ref, k_ref, vt_ref, og_ref, gt_ref, cn0_ref, m0_ref, hs_ref, row_ref, cn_ref, colb_ref, hacc_ref = refs
    else:
        k_ref, vt_ref, gt_ref, cn0_ref, m0_ref, cn_out_ref, m_out_ref, row_ref, cn_ref = refs

    parts = [[], [], [], []]
    for c in range(nc):
        blk = gt_ref[c * CHUNK:(c + 1) * CHUNK, :].T
        for q in range(4):
            parts[q].append(blk[q * HEADS:(q + 1) * HEADS])
    ig = [jnp.concatenate(parts[0], axis=0), jnp.concatenate(parts[1], axis=0)]
    lf = [_log_sigmoid(jnp.concatenate(parts[2], axis=0)), _log_sigmoid(jnp.concatenate(parts[3], axis=0))]
    full = (nc * HEADS, CHUNK)
    betas = []

    for dr in range(2):
        rev = dr == 1
        b = _lane_scan(lf[dr], jnp.add, 0.0, rev)
        edge = 0 if rev else CHUNK - 1
        bl = jnp.broadcast_to(b[:, edge:edge + 1], full)
        g = bl - b + ig[dr]
        gm = jnp.broadcast_to(jnp.max(g, axis=1, keepdims=True), full)
        wend = jnp.exp(g - gm)
        beta = ig[dr] - b
        betas.append(beta)
        cm = _lane_scan(beta, jnp.maximum, -jnp.inf, rev)
        m = m0_ref[dr]
        mprev, aa, ee = [None] * nc, [None] * nc, [None] * nc
        for c in (reversed(range(nc)) if rev else range(nc)):
            sl = slice(c * HEADS, (c + 1) * HEADS)
            m_new = jnp.maximum(bl[sl] + m, gm[sl])
            mprev[c] = m
            aa[c] = jnp.exp(bl[sl] + m - m_new)
            ee[c] = jnp.exp(gm[sl] - m_new)
            m = m_new
        if not with_output:
            m_out_ref[dr] = m
        mprev = jnp.concatenate(mprev, axis=0)
        alpha = -jnp.maximum(mprev, cm)
        shape3 = (nc, HEADS, CHUNK)
        row_ref[dr, _ROW_ALPHA] = alpha.reshape(shape3)
        row_ref[dr, _ROW_AT] = jnp.exp(mprev + alpha).reshape(shape3)
        row_ref[dr, _ROW_EM] = jnp.exp(alpha - b).reshape(shape3)
        row_ref[dr, _ROW_WEND] = wend.reshape(shape3)
        row_ref[dr, _ROW_A] = jnp.concatenate(aa, axis=0).reshape(shape3)
        row_ref[dr, _ROW_E] = jnp.concatenate(ee, axis=0).reshape(shape3)

    if with_output:
        pad = jnp.zeros((LANES - 2 * HEADS, CHUNK), F32)
        for c in range(nc):
            sl = slice(c * HEADS, (c + 1) * HEADS)
            colb_ref[c] = jnp.concatenate([betas[0][sl], betas[1][sl], pad], axis=0).T

    cn_ref[...] = cn0_ref[...]
    s_idx = lax.broadcasted_iota(jnp.int32, (CHUNK, CHUNK), 0)
    t_idx = lax.broadcasted_iota(jnp.int32, (CHUNK, CHUNK), 1)
    first_row = lax.broadcasted_iota(jnp.int32, (STATE_ROWS - V_DIM, CHUNK), 0) == 0
    zeros_half = jnp.zeros((QK_DIM, CHUNK), BF16)

    def chunk_step(c, dr):
        r0 = pl.multiple_of(c * CHUNK, CHUNK)

        def row(qi, h):
            return row_ref[dr, qi, c, h:h + 1, :]

        for h in range(HEADS):
            hs_ = slice(h * V_DIM, (h + 1) * V_DIM)
            pair = slice((h // 2) * LANES, (h // 2 + 1) * LANES)
            kp = k_ref[pl.ds(r0, CHUNK), pair]
            vt = vt_ref[c, hs_, :]
            cn = cn_ref[dr, h]

            if with_output:
                qt = qt_ref[c, h * QK_DIM:(h + 1) * QK_DIM, :]
                qtp = jnp.concatenate([zeros_half, qt] if h % 2 else [qt, zeros_half], axis=0)
                st = _dot(kp, qtp)
                beta = colb_ref[c, :, dr * HEADS + h:dr * HEADS + h + 1]
                mask = (s_idx <= t_idx) if dr == 0 else (s_idx >= t_idx)
                pt = st * jnp.where(mask, jnp.exp(beta + row(_ROW_ALPHA, h)), 0.0)
                inter = _dot(cn.astype(BF16), qtp)
                at = row(_ROW_AT, h)
                num = _dot(vt, pt.astype(BF16)) + at * inter[:V_DIM]
                den = jnp.sum(pt, axis=0, keepdims=True) + at * inter[V_DIM:V_DIM + 1]
                ht = num * (1.0 / jnp.maximum(jnp.abs(den), row(_ROW_EM, h)))
                if dr == 0:
                    hacc_ref[c, hs_, :] = ht
                else:
                    tot = (hacc_ref[c, hs_, :] + ht).T
                    hs_ref[pl.ds(r0, CHUNK), hs_] = (og_ref[pl.ds(r0, CHUNK), hs_].astype(F32) * tot).astype(BF16)

            wend = row(_ROW_WEND, h)
            tail = jnp.where(first_row, wend, 0.0)
            lhs = jnp.concatenate([vt.astype(F32) * wend, tail], axis=0).astype(BF16)
            cc = _dot(lhs, kp)
            cn_ref[dr, h] = row(_ROW_A, h) * cn + row(_ROW_E, h) * cc

    def fwd_body(i, carry):
        chunk_step(i, 0)
        return carry

    def bwd_body(i, carry):
        chunk_step(nc - 1 - i, 1)
        return carry

    lax.fori_loop(0, nc, fwd_body, 0)
    lax.fori_loop(0, nc, bwd_body, 0)
    if not with_output:
        cn_out_ref[...] = cn_ref[...]


def _core_layer(qt, k, vt, og, gt, cn0, m0, with_output):
    b, seq, _ = k.shape
    d = vt.shape[2]
    nc = seq // CHUNK
    tok = lambda w: pl.BlockSpec((None, seq, w), lambda i: (i, 0, 0))
    slab = lambda r: pl.BlockSpec((None, nc, r, CHUNK), lambda i: (i, 0, 0, 0))
    cn_spec = pl.BlockSpec((None, 2, HEADS, STATE_ROWS, LANES), lambda i: (i, 0, 0, 0, 0))
    m_spec = pl.BlockSpec((None, 2, HEADS, LANES), lambda i: (i, 0, 0, 0))
    scratch = [pltpu.VMEM((2, 6, nc, HEADS, CHUNK), F32),
               pltpu.VMEM((2, HEADS, STATE_ROWS, LANES), F32)]
    if with_output:
        ins = (qt, k, vt, og, gt, cn0, m0)
        in_specs = [slab(QK_WIDTH), tok(QK_WIDTH), slab(d), tok(d), tok(LANES), cn_spec, m_spec]
        out_specs = tok(d)
        out_shape = jax.ShapeDtypeStruct((b, seq, d), BF16)
        scratch = scratch + [pltpu.VMEM((nc, CHUNK, LANES), F32),
                             pltpu.VMEM((nc, d, CHUNK), F32)]
    else:
        ins = (k, vt, gt, cn0, m0)
        in_specs = [tok(QK_WIDTH), slab(d), tok(LANES), cn_spec, m_spec]
        out_specs = [cn_spec, m_spec]
        out_shape = [jax.ShapeDtypeStruct(cn0.shape, F32), jax.ShapeDtypeStruct(m0.shape, F32)]
    return pl.pallas_call(
        functools.partial(_core_kernel, seq=seq, with_output=with_output),
        grid=(b,),
        in_specs=in_specs,
        out_specs=out_specs,
        out_shape=out_shape,
        scratch_shapes=scratch,
        compiler_params=_cparams(("parallel",)),
        name="mlstm_core",
    )(*ins)


def _outproj_kernel(x_ref, hs_ref, m_ref, ng_ref, w_ref, o_ref):
    y = _dot(hs_ref[...], w_ref[...])
    o_ref[...] = x_ref[...] + m_ref[...][2:3] * _rms(y, ng_ref[...][1:2])


def _outproj_layer(x, hs, mods, ng, w_out, tm):
    b, n, d = x.shape
    tile = pl.BlockSpec((None, tm, d), lambda i, t: (i, t, 0))
    return pl.pallas_call(
        _outproj_kernel,
        grid=(b, n // tm),
        in_specs=[tile, tile, _mod_spec(mods, d), _const_spec(ng), _const_spec(w_out)],
        out_specs=tile,
        out_shape=jax.ShapeDtypeStruct(x.shape, F32),
        compiler_params=_cparams(("parallel", "parallel")),
        name="mlstm_outproj",
    )(x, hs, mods, ng, w_out)


def kernel(x, c, ctx, c_ctx, mod_w, mod_b, norm_g, pool_w, pool_scale, mlstm_w_in, mlstm_b_gate, mlstm_conv_w,
           mlstm_conv_b, mlstm_w_out, ffn_w_in, ffn_conv_w, ffn_conv_b, ffn_w_out):
    b, n, d = x.shape
    ctx_len = ctx.shape[1]
    cond = jnp.zeros((COND_ROWS, d), F32).at[:b].set(c).at[b].set(c_ctx)
    mods = _modulation(cond, mod_w, mod_b).reshape(mod_w.shape[0], COND_ROWS, N_MOD, d)
    ctx_flat = ctx.reshape(1, b * ctx_len, d)

    def ffn_weights(i):
        w_in = ffn_w_in[i].astype(BF16)
        return (w_in[:, D_FF:], w_in[:, :D_FF], ffn_conv_w[i].reshape(9, D_FF), ffn_conv_b[i].reshape(1, D_FF),
                ffn_w_out[i].astype(BF16))

    mx, mc = mods[0, :b], mods[0, b:b + 1]
    pw = pool_w[0].astype(BF16)
    ps = pool_scale[0].reshape(1, d)
    x = _pool_layer(x, mx, norm_g[0], pw, ps, GRID_W, 512)
    ctx_flat = _pool_layer(ctx_flat, mc, norm_g[0], pw, ps, ctx_len, 512)
    fw = ffn_weights(0)
    x = _ffn_layer(x, mx, norm_g[0], *fw, row_w=GRID_W, tm=1024, vertical=True)
    ctx_flat = _ffn_layer(ctx_flat, mc, norm_g[0], *fw, row_w=ctx_len, tm=1024, vertical=False)

    mx, mc = mods[1, :b], mods[1, b:b + 1]
    w_in = mlstm_w_in[0]
    qkw = 2 * QK_WIDTH
    wqk = w_in[:, :qkw].astype(BF16)
    wvt = w_in[:, qkw:qkw + d].T.astype(BF16)
    wo = w_in[:, qkw + d:qkw + 2 * d].astype(BF16)
    wgt = jnp.zeros((d, LANES), F32).at[:, :4 * HEADS].set(w_in[:, qkw + 2 * d:]).astype(BF16)
    bgate = jnp.zeros((1, LANES), F32).at[0, :4 * HEADS].set(mlstm_b_gate[0])
    pargs = (norm_g[1], wqk, wvt, wo, wgt, mlstm_conv_w[0].reshape(9, qkw), mlstm_conv_b[0].reshape(1, qkw), bgate)

    _, k_c, vt_c, _, gt_c = _proj_layer(ctx_flat, mc, *pargs, row_w=ctx_len, tm=1024, vertical=False)
    cn0 = jnp.zeros((b, 2, HEADS, STATE_ROWS, LANES), F32)
    m0 = jnp.zeros((b, 2, HEADS, LANES), F32)
    cn1, m1 = _core_layer(None, k_c.reshape(b, ctx_len, QK_WIDTH), vt_c.reshape(b, ctx_len // CHUNK, d, CHUNK), None,
                          gt_c.reshape(b, ctx_len, LANES), cn0, m0, with_output=False)

    qt, k, vt, og, gt = _proj_layer(x, mx, *pargs, row_w=GRID_W, tm=1024, vertical=True)
    hs = _core_layer(qt, k, vt, og, gt, cn1, m1, with_output=True)
    x = _outproj_layer(x, hs, mx, norm_g[1], mlstm_w_out[0].astype(BF16), 1024)
    x = _ffn_layer(x, mx, norm_g[1], *ffn_weights(1), row_w=GRID_W, tm=1024, vertical=True)
    return x
```

```python
import functools

import numpy as np
import jax
import jax.numpy as jnp
from jax import lax
from jax.experimental import pallas as pl
from jax.experimental.pallas import tpu as pltpu

F32 = jnp.float32
BF16 = jnp.bfloat16

D_MODEL = 1024
GRID_W = 64
EPS = 1e-6
N_MOD = 6
POOL_WINDOWS = (2, 4, 8, 16)
POOL_GROUP_DIM = D_MODEL // len(POOL_WINDOWS)
HEADS = 8
V_DIM = D_MODEL // HEADS
QK_DIM = V_DIM // 2
QK_WIDTH = HEADS * QK_DIM
D_FF = ((8 * D_MODEL // 3 + 255) // 256) * 256
COND_ROWS = 16
LANES = 128
CHUNK = LANES
FF_COLS = 256
VMEM_LIMIT = 56 * 1024 * 1024


def _cparams(sem):
    return pltpu.CompilerParams(dimension_semantics=sem, vmem_limit_bytes=VMEM_LIMIT)


def _sigmoid(x):
    return 0.5 * jnp.tanh(0.5 * x) + 0.5


def _silu(x):
    hx = 0.5 * x
    return hx + hx * jnp.tanh(hx)


def _rms(xf, g):
    ms = jnp.mean(xf * xf, axis=-1, keepdims=True)
    return xf * lax.rsqrt(ms + EPS) * g


def _modulate(xf, g, shift, scale):
    return _rms(xf, g * (1.0 + scale)) + shift


def _dot(a, b):
    return jnp.dot(a, b, preferred_element_type=F32)


def _layer_spec(a, layer):
    nd = a.ndim - 1
    return pl.BlockSpec((None,) + a.shape[1:], lambda i, t: (layer,) + (0,) * nd, pipeline_mode=pl.Buffered(1))


def _const_spec(a):
    nd = a.ndim
    return pl.BlockSpec(a.shape, lambda i, t: (0,) * nd, pipeline_mode=pl.Buffered(1))


def _mod_spec(mods, layer, mod_row):
    blk = (None, None) + mods.shape[2:]
    if mod_row is None:
        return pl.BlockSpec(blk, lambda i, t: (layer, i, 0, 0))
    return pl.BlockSpec(blk, lambda i, t: (layer, mod_row, 0, 0))


def _tile_specs(d, tm, row_w, vertical, n):
    main = pl.BlockSpec((None, tm, d), lambda i, t: (i, t, 0))
    if not vertical:
        return [main]
    r = tm // row_w
    n_rows = n // row_w
    top = pl.BlockSpec((None, row_w, d), lambda i, t: (i, jnp.maximum(t * r - 1, 0), 0))
    bot = pl.BlockSpec((None, row_w, d), lambda i, t: (i, jnp.minimum((t + 1) * r, n_rows - 1), 0))
    return [main, top, bot]


def _mod_kernel(c_ref, w_ref, b_ref, o_ref):
    s = _silu(c_ref[...]).astype(BF16)
    o_ref[...] = _dot(s, w_ref[...].astype(BF16)) + b_ref[...]


def _modulation(cond, mod_w, mod_b):
    depth, d, n = mod_w.shape
    tn = 1536
    return pl.pallas_call(
        _mod_kernel,
        grid=(depth, n // tn),
        in_specs=[pl.BlockSpec((COND_ROWS, d), lambda i, j: (0, 0)),
                  pl.BlockSpec((None, d, tn), lambda i, j: (i, 0, j)),
                  pl.BlockSpec((None, 1, tn), lambda i, j: (i, 0, j))],
        out_specs=pl.BlockSpec((None, COND_ROWS, tn), lambda i, j: (i, 0, j)),
        out_shape=jax.ShapeDtypeStruct((depth, COND_ROWS, n), F32),
        compiler_params=_cparams(("parallel", "parallel")),
        name="modulation",
    )(cond, mod_w, mod_b.reshape(depth, 1, n))


def _pool_inv_counts(row_w, tn):
    pos = np.arange(tn) % row_w
    rows = []
    for win in POOL_WINDOWS:
        lo = np.clip(pos - win // 2, 0, row_w)
        hi = np.clip(pos - win // 2 + win, 0, row_w)
        rows.append((1.0 / (hi - lo)).astype(np.float32))
    return jnp.asarray(np.broadcast_to(np.stack(rows)[:, :, None], (len(POOL_WINDOWS), tn, LANES)))


def _pool_kernel(x_ref, m_ref, ng_ref, pw_ref, ps_ref, ic_ref, o_ref, *, row_w):
    x = x_ref[...]
    m = m_ref[...]
    ng = ng_ref[...]
    h = _modulate(x, ng[0:1], m[0:1], m[1:2])
    tn = x.shape[0]
    gd = POOL_GROUP_DIM
    pos = lax.broadcasted_iota(jnp.int32, (tn, gd), 0) & (row_w - 1)

    def shifted(a, d):
        rolled = pltpu.roll(a, d % tn, axis=0)
        valid = (pos >= d) if d > 0 else (pos < row_w + d)
        return jnp.where(valid, rolled, 0.0)

    ys = []
    for gi, win in enumerate(POOL_WINDOWS):
        hg = h[:, gi * gd:(gi + 1) * gd]
        half = win // 2
        back, fwd, k = hg, hg, 1
        while k < half:
            back = back + shifted(back, k)
            fwd = fwd + shifted(fwd, -k)
            k *= 2
        total = shifted(back, 1) + fwd
        inv_cnt = ic_ref[gi]
        y = total * jnp.concatenate([inv_cnt] * (gd // LANES), axis=1) - hg
        ys.append(_dot(y.astype(BF16), pw_ref[gi]))
    y = jnp.concatenate(ys, axis=1) * ps_ref[...]
    o_ref[...] = x + _rms(y, ng[1:2] * m[2:3])


def _pool_layer(x, mods, layer, mod_row, norm_g, pool_w, pool_scale, row_w, tn):
    b, n, d = x.shape
    inv_cnt = _pool_inv_counts(row_w, tn)
    tile = pl.BlockSpec((None, tn, d), lambda i, t: (i, t, 0))
    return pl.pallas_call(
        functools.partial(_pool_kernel, row_w=row_w),
        grid=(b, n // tn),
        in_specs=[tile, _mod_spec(mods, layer, mod_row), _layer_spec(norm_g, layer), _layer_spec(pool_w, 0),
                  _layer_spec(pool_scale, 0), _const_spec(inv_cnt)],
        out_specs=tile,
        out_shape=jax.ShapeDtypeStruct(x.shape, F32),
        compiler_params=_cparams(("parallel", "parallel")),
        name="pool_mixer",
    )(x, mods, norm_g, pool_w, pool_scale, inv_cnt)


def _conv3x3(g, w9, bias, row_w, tm, vertical):
    rows = g.shape[0]
    pos = lax.broadcasted_iota(jnp.int32, g.shape, 0) & (row_w - 1)
    left = jnp.where(pos >= 1, pltpu.roll(g, 1, axis=0), 0.0)
    right = jnp.where(pos < row_w - 1, pltpu.roll(g, rows - 1, axis=0), 0.0)
    acc = None
    for dr in ((-1, 0, 1) if vertical else (0,)):
        off = (row_w if vertical else 0) + dr * row_w
        for dc, arr in ((-1, left), (0, g), (1, right)):
            tap = (dr + 1) * 3 + (dc + 1)
            term = w9[tap:tap + 1] * arr[off:off + tm]
            acc = term if acc is None else acc + term
    return acc + bias


def _fill_modulated(hh_ref, x, halo_refs, ng_row, shift, scale, row_w, tm):
    gain = ng_row * (1.0 + scale)
    if halo_refs is None:
        hh_ref[...] = (_rms(x, gain) + shift).astype(BF16)
        return
    top_ref, bot_ref = halo_refs
    t = pl.program_id(1)
    last = pl.num_programs(1) - 1
    top = (_rms(top_ref[...], gain) + shift) * (t > 0).astype(F32)
    bot = (_rms(bot_ref[...], gain) + shift) * (t < last).astype(F32)
    hh_ref[0:row_w] = top.astype(BF16)
    hh_ref[row_w:row_w + tm] = (_rms(x, gain) + shift).astype(BF16)
    hh_ref[row_w + tm:row_w + tm + row_w] = bot.astype(BF16)


def _split_tile_refs(refs, vertical):
    if vertical:
        return refs[0], (refs[1], refs[2]), refs[3:]
    return refs[0], None, refs[1:]


def _ffn_kernel(*refs, row_w, tm, vertical):
    x_ref, halo, rest = _split_tile_refs(refs, vertical)
    m_ref, ng_ref, win_ref, cw_ref, cb_ref, wo_ref, o_ref, hh_ref, a_ref = rest
    x = x_ref[...]
    m = m_ref[...]
    ng = ng_ref[...]
    _fill_modulated(hh_ref, x, halo, ng[2:3], m[3:4], m[4:5], row_w, tm)
    lo = row_w if vertical else 0
    for j in range(D_FF // FF_COLS):
        cs = slice(j * FF_COLS, (j + 1) * FF_COLS)
        gate_cs = slice(D_FF + j * FF_COLS, D_FF + (j + 1) * FF_COLS)
        g = _dot(hh_ref[...], win_ref[:, gate_cs])
        cv = _conv3x3(g, cw_ref[:, cs], cb_ref[:, cs], row_w, tm, vertical)
        u = _dot(hh_ref[lo:lo + tm], win_ref[:, cs])
        a_ref[:, cs] = (_silu(cv) * u).astype(BF16)
    f = _dot(a_ref[...], wo_ref[...])
    o_ref[...] = x + _rms(f, ng[3:4] * m[5:6])


def _ffn_layer(x, mods, layer, mod_row, norm_g, w_in, conv_w, conv_b, w_out, row_w, tm, vertical):
    b, n, d = x.shape
    tmh = tm + 2 * row_w if vertical else tm
    x_specs = _tile_specs(d, tm, row_w, vertical, n)
    params = (norm_g, w_in, conv_w, conv_b, w_out)
    return pl.pallas_call(
        functools.partial(_ffn_kernel, row_w=row_w, tm=tm, vertical=vertical),
        grid=(b, n // tm),
        in_specs=x_specs + [_mod_spec(mods, layer, mod_row)] + [_layer_spec(a, layer) for a in params],
        out_specs=pl.BlockSpec((None, tm, d), lambda i, t: (i, t, 0)),
        out_shape=jax.ShapeDtypeStruct(x.shape, F32),
        scratch_shapes=[pltpu.VMEM((tmh, d), BF16), pltpu.VMEM((tm, D_FF), BF16)],
        compiler_params=_cparams(("parallel", "parallel")),
        name="conv_ffn",
    )(*([x] * len(x_specs)), mods, *params)


def _proj_kernel(*refs, row_w, tm, vertical):
    x_ref, halo, rest = _split_tile_refs(refs, vertical)
    (m_ref, ng_ref, win_ref, cw_ref, cb_ref, wvt_ref, wgt_ref, bg_ref,
     qt_ref, k_ref, vt_ref, og_ref, gt_ref, hh_ref) = rest
    m = m_ref[...]
    ng = ng_ref[...]
    _fill_modulated(hh_ref, x_ref[...], halo, ng[0:1], m[0:1], m[1:2], row_w, tm)
    lo = row_w if vertical else 0
    n_chunks = tm // CHUNK
    d = hh_ref.shape[1]
    for j in range(2 * QK_WIDTH // FF_COLS):
        cs = slice(j * FF_COLS, (j + 1) * FF_COLS)
        g = _dot(hh_ref[...], win_ref[:, cs])
        act = _silu(_conv3x3(g, cw_ref[:, cs], cb_ref[:, cs], row_w, tm, vertical))
        if j < QK_WIDTH // FF_COLS:
            for ci in range(n_chunks):
                qt_ref[ci, cs, :] = act[ci * CHUNK:(ci + 1) * CHUNK].T.astype(BF16)
        else:
            k_ref[:, j * FF_COLS - QK_WIDTH:(j + 1) * FF_COLS - QK_WIDTH] = (act * QK_DIM ** -0.5).astype(BF16)
    h = hh_ref[lo:lo + tm]
    vt = lax.dot_general(wvt_ref[...], h, (((1,), (1,)), ((), ())), preferred_element_type=F32)
    for ci in range(n_chunks):
        vt_ref[ci] = vt[:, ci * CHUNK:(ci + 1) * CHUNK].astype(BF16)
    o_cols = slice(2 * QK_WIDTH + d, 2 * QK_WIDTH + 2 * d)
    og_ref[...] = _sigmoid(_dot(h, win_ref[:, o_cols])).astype(BF16)
    gt_ref[...] = _dot(h, wgt_ref[...]) + bg_ref[...]


def _proj_layer(x, mods, layer, mod_row, norm_g, w_in, conv_w, conv_b, wvt, wgt, bgate, row_w, tm, vertical):
    b, n, d = x.shape
    tmh = tm + 2 * row_w if vertical else tm
    x_specs = _tile_specs(d, tm, row_w, vertical, n)
    tile = lambda w: pl.BlockSpec((None, tm, w), lambda i, t: (i, t, 0))
    slab = lambda r: pl.BlockSpec((None, tm // CHUNK, r, CHUNK), lambda i, t: (i, t, 0, 0))
    nc = n // CHUNK
    return pl.pallas_call(
        functools.partial(_proj_kernel, row_w=row_w, tm=tm, vertical=vertical),
        grid=(b, n // tm),
        in_specs=(x_specs + [_mod_spec(mods, layer, mod_row), _layer_spec(norm_g, layer)]
                  + [_layer_spec(a, 0) for a in (w_in, conv_w, conv_b)] + [_const_spec(a) for a in (wvt, wgt, bgate)]),
        out_specs=[slab(QK_WIDTH), tile(QK_WIDTH), slab(d), tile(d), tile(LANES)],
        out_shape=[jax.ShapeDtypeStruct((b, nc, QK_WIDTH, CHUNK), BF16), jax.ShapeDtypeStruct((b, n, QK_WIDTH), BF16),
                   jax.ShapeDtypeStruct((b, nc, d, CHUNK), BF16), jax.ShapeDtypeStruct((b, n, d), BF16),
                   jax.ShapeDtypeStruct((b, n, LANES), F32)],
        scratch_shapes=[pltpu.VMEM((tmh, d), BF16)],
        compiler_params=_cparams(("parallel", "parallel")),
        name="mlstm_proj",
    )(*([x] * len(x_specs)), mods, norm_g, w_in, conv_w, conv_b, wvt, wgt, bgate)


def _lane_scan(x, op, ident, reverse):
    lane = lax.broadcasted_iota(jnp.int32, x.shape, 1)
    d = 1
    while d < LANES:
        if reverse:
            moved = jnp.where(lane < LANES - d, pltpu.roll(x, LANES - d, axis=1), ident)
        else:
            moved = jnp.where(lane >= d, pltpu.roll(x, d, axis=1), ident)
        x = op(x, moved)
        d *= 2
    return x


def _log_sigmoid(x):
    return jnp.minimum(x, 0.0) - jnp.log(1.0 + jnp.exp(-jnp.abs(x)))


_ROW_ALPHA, _ROW_AT, _ROW_EM, _ROW_WEND, _ROW_A, _ROW_E = range(6)
STATE_ROWS = V_DIM + 16


def _core_kernel(*refs, seq, context_pass):
    nc = seq // CHUNK
    if context_pass:
        k_ref, vt_ref, gt_ref, cn_out_ref, m_out_ref, row_ref, cn_ref = refs
    else:
        qt_ref, k_ref, vt_ref, og_ref, gt_ref, cn0_ref, m0_ref, hs_ref, row_ref, cn_ref, colb_ref, hacc_ref = refs

    parts = [[], [], [], []]
    for c in range(nc):
        blk = gt_ref[c * CHUNK:(c + 1) * CHUNK, :].T
        for q in range(4):
            parts[q].append(blk[q * HEADS:(q + 1) * HEADS])
    ig = [jnp.concatenate(parts[0], axis=0), jnp.concatenate(parts[1], axis=0)]
    lf = [_log_sigmoid(jnp.concatenate(parts[2], axis=0)), _log_sigmoid(jnp.concatenate(parts[3], axis=0))]
    full = (nc * HEADS, CHUNK)
    betas = []

    for dr in range(2):
        rev = dr == 1
        b = _lane_scan(lf[dr], jnp.add, 0.0, rev)
        edge = 0 if rev else CHUNK - 1
        bl = jnp.broadcast_to(b[:, edge:edge + 1], full)
        g = bl - b + ig[dr]
        gm = jnp.broadcast_to(jnp.max(g, axis=1, keepdims=True), full)
        wend = jnp.exp(g - gm)
        beta = ig[dr] - b
        betas.append(beta)
        cm = _lane_scan(beta, jnp.maximum, -jnp.inf, rev)
        m = jnp.zeros((HEADS, LANES), F32) if context_pass else m0_ref[dr]
        mprev, aa, ee = [None] * nc, [None] * nc, [None] * nc
        for c in (reversed(range(nc)) if rev else range(nc)):
            sl = slice(c * HEADS, (c + 1) * HEADS)
            m_new = jnp.maximum(bl[sl] + m, gm[sl])
            mprev[c] = m
            aa[c] = jnp.exp(bl[sl] + m - m_new)
            ee[c] = jnp.exp(gm[sl] - m_new)
            m = m_new
        if context_pass:
            m_out_ref[dr] = m
        mprev = jnp.concatenate(mprev, axis=0)
        alpha = -jnp.maximum(mprev, cm)
        shape3 = (nc, HEADS, CHUNK)
        row_ref[dr, _ROW_ALPHA] = alpha.reshape(shape3)
        row_ref[dr, _ROW_AT] = jnp.exp(mprev + alpha).reshape(shape3)
        row_ref[dr, _ROW_EM] = jnp.exp(alpha - b).reshape(shape3)
        row_ref[dr, _ROW_WEND] = wend.reshape(shape3)
        row_ref[dr, _ROW_A] = jnp.concatenate(aa, axis=0).reshape(shape3)
        row_ref[dr, _ROW_E] = jnp.concatenate(ee, axis=0).reshape(shape3)

    if context_pass:
        cn_ref[...] = jnp.zeros(cn_ref.shape, F32)
    else:
        cn_ref[...] = cn0_ref[...]
        pad = jnp.zeros((LANES - 2 * HEADS, CHUNK), F32)
        for c in range(nc):
            sl = slice(c * HEADS, (c + 1) * HEADS)
            colb_ref[c] = jnp.concatenate([betas[0][sl], betas[1][sl], pad], axis=0).T

    s_idx = lax.broadcasted_iota(jnp.int32, (CHUNK, CHUNK), 0)
    t_idx = lax.broadcasted_iota(jnp.int32, (CHUNK, CHUNK), 1)
    first_row = lax.broadcasted_iota(jnp.int32, (STATE_ROWS - V_DIM, CHUNK), 0) == 0
    zeros_half = jnp.zeros((QK_DIM, CHUNK), BF16)

    def chunk_step(c, dr):
        r0 = pl.multiple_of(c * CHUNK, CHUNK)

        def row(qi, h):
            return row_ref[dr, qi, c, h:h + 1, :]

        for h in range(HEADS):
            hs_ = slice(h * V_DIM, (h + 1) * V_DIM)
            pair = slice((h // 2) * LANES, (h // 2 + 1) * LANES)
            kp = k_ref[pl.ds(r0, CHUNK), pair]
            vt = vt_ref[c, hs_, :]
            cn = cn_ref[dr, h]

            if not context_pass:
                qt = qt_ref[c, h * QK_DIM:(h + 1) * QK_DIM, :]
                qtp = jnp.concatenate([zeros_half, qt] if h % 2 else [qt, zeros_half], axis=0)
                st = _dot(kp, qtp)
                beta = colb_ref[c, :, dr * HEADS + h:dr * HEADS + h + 1]
                mask = (s_idx <= t_idx) if dr == 0 else (s_idx >= t_idx)
                pt = st * jnp.where(mask, jnp.exp(beta + row(_ROW_ALPHA, h)), 0.0)
                inter = _dot(cn.astype(BF16), qtp)
                at = row(_ROW_AT, h)
                num = _dot(vt, pt.astype(BF16)) + at * inter[:V_DIM]
                den = jnp.sum(pt, axis=0, keepdims=True) + at * inter[V_DIM:V_DIM + 1]
                ht = num * (1.0 / jnp.maximum(jnp.abs(den), row(_ROW_EM, h)))
                if dr == 0:
                    hacc_ref[c, hs_, :] = ht
                else:
                    tot = (hacc_ref[c, hs_, :] + ht).T
                    hs_ref[pl.ds(r0, CHUNK), hs_] = (og_ref[pl.ds(r0, CHUNK), hs_].astype(F32) * tot).astype(BF16)

            wend = row(_ROW_WEND, h)
            tail = jnp.where(first_row, wend, 0.0)
            lhs = jnp.concatenate([vt.astype(F32) * wend, tail], axis=0).astype(BF16)
            cc = _dot(lhs, kp)
            cn_ref[dr, h] = row(_ROW_A, h) * cn + row(_ROW_E, h) * cc

    def fwd_body(i, carry):
        chunk_step(i, 0)
        return carry

    def bwd_body(i, carry):
        chunk_step(nc - 1 - i, 1)
        return carry

    lax.fori_loop(0, nc, fwd_body, 0)
    lax.fori_loop(0, nc, bwd_body, 0)
    if context_pass:
        cn_out_ref[...] = cn_ref[...]


def _core_layer(k, vt, gt, qt=None, og=None, states=None):
    b, seq, _ = k.shape
    d = vt.shape[2]
    nc = seq // CHUNK
    context_pass = states is None
    tok = lambda w: pl.BlockSpec((None, seq, w), lambda i: (i, 0, 0))
    slab = lambda r: pl.BlockSpec((None, nc, r, CHUNK), lambda i: (i, 0, 0, 0))
    cn_spec = pl.BlockSpec((None, 2, HEADS, STATE_ROWS, LANES), lambda i: (i, 0, 0, 0, 0))
    m_spec = pl.BlockSpec((None, 2, HEADS, LANES), lambda i: (i, 0, 0, 0))
    scratch = [pltpu.VMEM((2, 6, nc, HEADS, CHUNK), F32),
               pltpu.VMEM((2, HEADS, STATE_ROWS, LANES), F32)]
    if context_pass:
        ins = (k, vt, gt)
        in_specs = [tok(QK_WIDTH), slab(d), tok(LANES)]
        out_specs = [cn_spec, m_spec]
        out_shape = [jax.ShapeDtypeStruct((b, 2, HEADS, STATE_ROWS, LANES), F32),
                     jax.ShapeDtypeStruct((b, 2, HEADS, LANES), F32)]
    else:
        ins = (qt, k, vt, og, gt) + tuple(states)
        in_specs = [slab(QK_WIDTH), tok(QK_WIDTH), slab(d), tok(d), tok(LANES), cn_spec, m_spec]
        out_specs = tok(d)
        out_shape = jax.ShapeDtypeStruct((b, seq, d), BF16)
        scratch = scratch + [pltpu.VMEM((nc, CHUNK, LANES), F32),
                             pltpu.VMEM((nc, d, CHUNK), F32)]
    return pl.pallas_call(
        functools.partial(_core_kernel, seq=seq, context_pass=context_pass),
        grid=(b,),
        in_specs=in_specs,
        out_specs=out_specs,
        out_shape=out_shape,
        scratch_shapes=scratch,
        compiler_params=_cparams(("parallel",)),
        name="mlstm_core",
    )(*ins)


def _outproj_kernel(x_ref, hs_ref, m_ref, ng_ref, w_ref, o_ref):
    y = _dot(hs_ref[...], w_ref[...])
    o_ref[...] = x_ref[...] + _rms(y, ng_ref[...][1:2] * m_ref[...][2:3])


def _outproj_layer(x, hs, mods, layer, norm_g, w_out, tm):
    b, n, d = x.shape
    tile = pl.BlockSpec((None, tm, d), lambda i, t: (i, t, 0))
    return pl.pallas_call(
        _outproj_kernel,
        grid=(b, n // tm),
        in_specs=[tile, tile, _mod_spec(mods, layer, None), _layer_spec(norm_g, layer), _layer_spec(w_out, 0)],
        out_specs=tile,
        out_shape=jax.ShapeDtypeStruct(x.shape, F32),
        compiler_params=_cparams(("parallel", "parallel")),
        name="mlstm_outproj",
    )(x, hs, mods, norm_g, w_out)


def kernel(x, c, ctx, c_ctx, mod_w, mod_b, norm_g, pool_w, pool_scale, mlstm_w_in, mlstm_b_gate, mlstm_conv_w,
           mlstm_conv_b, mlstm_w_out, ffn_w_in, ffn_conv_w, ffn_conv_b, ffn_w_out):
    b, n, d = x.shape
    ctx_len = ctx.shape[1]
    depth = mod_w.shape[0]
    cond = jnp.zeros((COND_ROWS, d), F32).at[:b].set(c).at[b].set(c_ctx)
    mods = _modulation(cond, mod_w, mod_b).reshape(depth, COND_ROWS, N_MOD, d)
    ctx_row = b
    ctx_flat = ctx.reshape(1, b * ctx_len, d)

    ffn = (ffn_w_in.astype(BF16), ffn_conv_w.reshape(depth, 9, D_FF), ffn_conv_b.reshape(depth, 1, D_FF),
           ffn_w_out.astype(BF16))

    pool = (norm_g, pool_w.astype(BF16), pool_scale.reshape(-1, 1, d))
    x = _pool_layer(x, mods, 0, None, *pool, row_w=GRID_W, tn=512)
    ctx_flat = _pool_layer(ctx_flat, mods, 0, ctx_row, *pool, row_w=ctx_len, tn=512)
    x = _ffn_layer(x, mods, 0, None, norm_g, *ffn, row_w=GRID_W, tm=1024, vertical=True)
    ctx_flat = _ffn_layer(ctx_flat, mods, 0, ctx_row, norm_g, *ffn, row_w=ctx_len, tm=1024, vertical=False)

    qkw = 2 * QK_WIDTH
    w_in = mlstm_w_in.astype(BF16)
    wvt = mlstm_w_in[0, :, qkw:qkw + d].T.astype(BF16)
    wgt = jnp.zeros((d, LANES), F32).at[:, :4 * HEADS].set(mlstm_w_in[0, :, qkw + 2 * d:]).astype(BF16)
    bgate = jnp.zeros((1, LANES), F32).at[0, :4 * HEADS].set(mlstm_b_gate[0])
    proj = (norm_g, w_in, mlstm_conv_w.reshape(-1, 9, qkw), mlstm_conv_b.reshape(-1, 1, qkw), wvt, wgt, bgate)

    _, k_c, vt_c, _, gt_c = _proj_layer(ctx_flat, mods, 1, ctx_row, *proj, row_w=ctx_len, tm=1024, vertical=False)
    states = _core_layer(k_c.reshape(b, ctx_len, QK_WIDTH), vt_c.reshape(b, ctx_len // CHUNK, d, CHUNK),
                         gt_c.reshape(b, ctx_len, LANES))

    qt, k, vt, og, gt = _proj_layer(x, mods, 1, None, *proj, row_w=GRID_W, tm=1024, vertical=True)
    hs = _core_layer(k, vt, gt, qt=qt, og=og, states=states)
    x = _outproj_layer(x, hs, mods, 1, norm_g, mlstm_w_out.astype(BF16), 1024)
    x = _ffn_layer(x, mods, 1, None, norm_g, *ffn, row_w=GRID_W, tm=1024, vertical=True)
    return x
```

```python
import functools

import numpy as np
import jax
import jax.numpy as jnp
from jax import lax
from jax.experimental import pallas as pl
from jax.experimental.pallas import tpu as pltpu

F32 = jnp.float32
BF16 = jnp.bfloat16

D_MODEL = 1024
GRID_W = 64
EPS = 1e-6
N_MOD = 6
POOL_WINDOWS = (2, 4, 8, 16)
POOL_GROUP_DIM = D_MODEL // len(POOL_WINDOWS)
HEADS = 8
V_DIM = D_MODEL // HEADS
QK_DIM = V_DIM // 2
QK_WIDTH = HEADS * QK_DIM
D_FF = ((8 * D_MODEL // 3 + 255) // 256) * 256
COND_ROWS = 16
LANES = 128
CHUNK = LANES
FF_COLS = 256
VMEM_LIMIT = 56 * 1024 * 1024


def _cparams(sem):
    return pltpu.CompilerParams(dimension_semantics=sem, vmem_limit_bytes=VMEM_LIMIT)


def _sigmoid(x):
    return 0.5 * jnp.tanh(0.5 * x) + 0.5


def _silu(x):
    hx = 0.5 * x
    return hx + hx * jnp.tanh(hx)


def _rms(xf, g):
    ms = jnp.mean(xf * xf, axis=-1, keepdims=True)
    return xf * lax.rsqrt(ms + EPS) * g


def _modulate(xf, g, shift, scale):
    return _rms(xf, g * (1.0 + scale)) + shift


def _dot(a, b):
    return jnp.dot(a, b, preferred_element_type=F32)


def _layer_spec(a, layer):
    nd = a.ndim - 1
    return pl.BlockSpec((None,) + a.shape[1:], lambda i, t: (layer,) + (0,) * nd, pipeline_mode=pl.Buffered(1))


def _const_spec(a):
    nd = a.ndim
    return pl.BlockSpec(a.shape, lambda i, t: (0,) * nd, pipeline_mode=pl.Buffered(1))


def _mod_spec(mods, layer, mod_row):
    blk = (None, None) + mods.shape[2:]
    if mod_row is None:
        return pl.BlockSpec(blk, lambda i, t: (layer, i, 0, 0))
    return pl.BlockSpec(blk, lambda i, t: (layer, mod_row, 0, 0))


def _tile_specs(d, tm, row_w, vertical, n):
    main = pl.BlockSpec((None, tm, d), lambda i, t: (i, t, 0))
    if not vertical:
        return [main]
    r = tm // row_w
    n_rows = n // row_w
    top = pl.BlockSpec((None, row_w, d), lambda i, t: (i, jnp.maximum(t * r - 1, 0), 0))
    bot = pl.BlockSpec((None, row_w, d), lambda i, t: (i, jnp.minimum((t + 1) * r, n_rows - 1), 0))
    return [main, top, bot]


def _mod_kernel(c_ref, w_ref, b_ref, o_ref):
    s = _silu(c_ref[...]).astype(BF16)
    o_ref[...] = _dot(s, w_ref[...].astype(BF16)) + b_ref[...]


def _modulation(cond, mod_w, mod_b):
    depth, d, n = mod_w.shape
    tn = 1536
    return pl.pallas_call(
        _mod_kernel,
        grid=(depth, n // tn),
        in_specs=[pl.BlockSpec((COND_ROWS, d), lambda i, j: (0, 0)),
                  pl.BlockSpec((None, d, tn), lambda i, j: (i, 0, j)),
                  pl.BlockSpec((None, 1, tn), lambda i, j: (i, 0, j))],
        out_specs=pl.BlockSpec((None, COND_ROWS, tn), lambda i, j: (i, 0, j)),
        out_shape=jax.ShapeDtypeStruct((depth, COND_ROWS, n), F32),
        compiler_params=_cparams(("parallel", "parallel")),
        name="modulation",
    )(cond, mod_w, mod_b.reshape(depth, 1, n))


def _pool_inv_counts(row_w, tn):
    pos = np.arange(tn) % row_w
    rows = []
    for win in POOL_WINDOWS:
        lo = np.clip(pos - win // 2, 0, row_w)
        hi = np.clip(pos - win // 2 + win, 0, row_w)
        rows.append((1.0 / (hi - lo)).astype(np.float32))
    return jnp.asarray(np.broadcast_to(np.stack(rows)[:, :, None], (len(POOL_WINDOWS), tn, LANES)))


def _pool_kernel(x_ref, m_ref, ng_ref, pw_ref, ps_ref, ic_ref, o_ref, *, row_w):
    x = x_ref[...]
    m = m_ref[...]
    ng = ng_ref[...]
    h = _modulate(x, ng[0:1], m[0:1], m[1:2])
    tn = x.shape[0]
    gd = POOL_GROUP_DIM
    pos = lax.broadcasted_iota(jnp.int32, (tn, gd), 0) & (row_w - 1)

    def shifted(a, d):
        rolled = pltpu.roll(a, d % tn, axis=0)
        valid = (pos >= d) if d > 0 else (pos < row_w + d)
        return jnp.where(valid, rolled, 0.0)

    ys = []
    for gi, win in enumerate(POOL_WINDOWS):
        hg = h[:, gi * gd:(gi + 1) * gd]
        half = win // 2
        back, fwd, k = hg, hg, 1
        while k < half:
            back = back + shifted(back, k)
            fwd = fwd + shifted(fwd, -k)
            k *= 2
        total = shifted(back, 1) + fwd
        inv_cnt = ic_ref[gi]
        y = total * jnp.concatenate([inv_cnt] * (gd // LANES), axis=1) - hg
        ys.append(_dot(y.astype(BF16), pw_ref[gi]))
    y = jnp.concatenate(ys, axis=1) * ps_ref[...]
    o_ref[...] = x + _rms(y, ng[1:2] * m[2:3])


def _pool_layer(x, mods, layer, mod_row, norm_g, pool_w, pool_scale, row_w, tn):
    b, n, d = x.shape
    inv_cnt = _pool_inv_counts(row_w, tn)
    tile = pl.BlockSpec((None, tn, d), lambda i, t: (i, t, 0))
    return pl.pallas_call(
        functools.partial(_pool_kernel, row_w=row_w),
        grid=(b, n // tn),
        in_specs=[tile, _mod_spec(mods, layer, mod_row), _layer_spec(norm_g, layer), _layer_spec(pool_w, 0),
                  _layer_spec(pool_scale, 0), _const_spec(inv_cnt)],
        out_specs=tile,
        out_shape=jax.ShapeDtypeStruct(x.shape, F32),
        compiler_params=_cparams(("parallel", "parallel")),
        name="pool_mixer",
    )(x, mods, norm_g, pool_w, pool_scale, inv_cnt)


def _conv3x3(g, w9, bias, row_w, tm, vertical):
    rows = g.shape[0]
    pos = lax.broadcasted_iota(jnp.int32, g.shape, 0) & (row_w - 1)
    left = jnp.where(pos >= 1, pltpu.roll(g, 1, axis=0), 0.0)
    right = jnp.where(pos < row_w - 1, pltpu.roll(g, rows - 1, axis=0), 0.0)
    acc = None
    for dr in ((-1, 0, 1) if vertical else (0,)):
        off = (row_w if vertical else 0) + dr * row_w
        for dc, arr in ((-1, left), (0, g), (1, right)):
            tap = (dr + 1) * 3 + (dc + 1)
            term = w9[tap:tap + 1] * arr[off:off + tm]
            acc = term if acc is None else acc + term
    return acc + bias


def _fill_modulated(hh_ref, x, halo_refs, ng_row, shift, scale, row_w, tm):
    gain = ng_row * (1.0 + scale)
    if halo_refs is None:
        hh_ref[...] = (_rms(x, gain) + shift).astype(BF16)
        return
    top_ref, bot_ref = halo_refs
    t = pl.program_id(1)
    last = pl.num_programs(1) - 1
    top = (_rms(top_ref[...], gain) + shift) * (t > 0).astype(F32)
    bot = (_rms(bot_ref[...], gain) + shift) * (t < last).astype(F32)
    hh_ref[0:row_w] = top.astype(BF16)
    hh_ref[row_w:row_w + tm] = (_rms(x, gain) + shift).astype(BF16)
    hh_ref[row_w + tm:row_w + tm + row_w] = bot.astype(BF16)


def _split_tile_refs(refs, vertical):
    if vertical:
        return refs[0], (refs[1], refs[2]), refs[3:]
    return refs[0], None, refs[1:]


def _ffn_kernel(*refs, row_w, tm, vertical):
    x_ref, halo, rest = _split_tile_refs(refs, vertical)
    m_ref, ng_ref, win_ref, cw_ref, cb_ref, wo_ref, o_ref, hh_ref, a_ref = rest
    x = x_ref[...]
    m = m_ref[...]
    ng = ng_ref[...]
    _fill_modulated(hh_ref, x, halo, ng[2:3], m[3:4], m[4:5], row_w, tm)
    lo = row_w if vertical else 0
    for j in range(D_FF // FF_COLS):
        cs = slice(j * FF_COLS, (j + 1) * FF_COLS)
        gate_cs = slice(D_FF + j * FF_COLS, D_FF + (j + 1) * FF_COLS)
        g = _dot(hh_ref[...], win_ref[:, gate_cs])
        cv = _conv3x3(g, cw_ref[:, cs], cb_ref[:, cs], row_w, tm, vertical)
        u = _dot(hh_ref[lo:lo + tm], win_ref[:, cs])
        a_ref[:, cs] = (_silu(cv) * u).astype(BF16)
    f = _dot(a_ref[...], wo_ref[...])
    o_ref[...] = x + _rms(f, ng[3:4] * m[5:6])


def _ffn_layer(x, mods, layer, mod_row, norm_g, w_in, conv_w, conv_b, w_out, row_w, tm, vertical):
    b, n, d = x.shape
    tmh = tm + 2 * row_w if vertical else tm
    x_specs = _tile_specs(d, tm, row_w, vertical, n)
    params = (norm_g, w_in, conv_w, conv_b, w_out)
    return pl.pallas_call(
        functools.partial(_ffn_kernel, row_w=row_w, tm=tm, vertical=vertical),
        grid=(b, n // tm),
        in_specs=x_specs + [_mod_spec(mods, layer, mod_row)] + [_layer_spec(a, layer) for a in params],
        out_specs=pl.BlockSpec((None, tm, d), lambda i, t: (i, t, 0)),
        out_shape=jax.ShapeDtypeStruct(x.shape, F32),
        scratch_shapes=[pltpu.VMEM((tmh, d), BF16), pltpu.VMEM((tm, D_FF), BF16)],
        compiler_params=_cparams(("parallel", "parallel")),
        name="conv_ffn",
    )(*([x] * len(x_specs)), mods, *params)


def _proj_kernel(*refs, row_w, tm, vertical):
    x_ref, halo, rest = _split_tile_refs(refs, vertical)
    (m_ref, ng_ref, win_ref, cw_ref, cb_ref, wvt_ref, wgt_ref, bg_ref,
     qt_ref, k_ref, vt_ref, og_ref, gt_ref, hh_ref) = rest
    m = m_ref[...]
    ng = ng_ref[...]
    _fill_modulated(hh_ref, x_ref[...], halo, ng[0:1], m[0:1], m[1:2], row_w, tm)
    lo = row_w if vertical else 0
    n_chunks = tm // CHUNK
    d = hh_ref.shape[1]
    for j in range(2 * QK_WIDTH // FF_COLS):
        cs = slice(j * FF_COLS, (j + 1) * FF_COLS)
        g = _dot(hh_ref[...], win_ref[:, cs])
        act = _silu(_conv3x3(g, cw_ref[:, cs], cb_ref[:, cs], row_w, tm, vertical))
        if j < QK_WIDTH // FF_COLS:
            for ci in range(n_chunks):
                qt_ref[ci, cs, :] = act[ci * CHUNK:(ci + 1) * CHUNK].T.astype(BF16)
        else:
            k_ref[:, j * FF_COLS - QK_WIDTH:(j + 1) * FF_COLS - QK_WIDTH] = (act * QK_DIM ** -0.5).astype(BF16)
    h = hh_ref[lo:lo + tm]
    vt = lax.dot_general(wvt_ref[...], h, (((1,), (1,)), ((), ())), preferred_element_type=F32)
    for ci in range(n_chunks):
        vt_ref[ci] = vt[:, ci * CHUNK:(ci + 1) * CHUNK].astype(BF16)
    o_cols = slice(2 * QK_WIDTH + d, 2 * QK_WIDTH + 2 * d)
    og_ref[...] = _sigmoid(_dot(h, win_ref[:, o_cols])).astype(BF16)
    gt_ref[...] = _dot(h, wgt_ref[...]) + bg_ref[...]


def _proj_layer(x, mods, layer, mod_row, norm_g, w_in, conv_w, conv_b, wvt, wgt, bgate, row_w, tm, vertical):
    b, n, d = x.shape
    tmh = tm + 2 * row_w if vertical else tm
    x_specs = _tile_specs(d, tm, row_w, vertical, n)
    tile = lambda w: pl.BlockSpec((None, tm, w), lambda i, t: (i, t, 0))
    slab = lambda r: pl.BlockSpec((None, tm // CHUNK, r, CHUNK), lambda i, t: (i, t, 0, 0))
    nc = n // CHUNK
    return pl.pallas_call(
        functools.partial(_proj_kernel, row_w=row_w, tm=tm, vertical=vertical),
        grid=(b, n // tm),
        in_specs=(x_specs + [_mod_spec(mods, layer, mod_row), _layer_spec(norm_g, layer)]
                  + [_layer_spec(a, 0) for a in (w_in, conv_w, conv_b)] + [_const_spec(a) for a in (wvt, wgt, bgate)]),
        out_specs=[slab(QK_WIDTH), tile(QK_WIDTH), slab(d), tile(d), tile(LANES)],
        out_shape=[jax.ShapeDtypeStruct((b, nc, QK_WIDTH, CHUNK), BF16), jax.ShapeDtypeStruct((b, n, QK_WIDTH), BF16),
                   jax.ShapeDtypeStruct((b, nc, d, CHUNK), BF16), jax.ShapeDtypeStruct((b, n, d), BF16),
                   jax.ShapeDtypeStruct((b, n, LANES), F32)],
        scratch_shapes=[pltpu.VMEM((tmh, d), BF16)],
        compiler_params=_cparams(("parallel", "parallel")),
        name="mlstm_proj",
    )(*([x] * len(x_specs)), mods, norm_g, w_in, conv_w, conv_b, wvt, wgt, bgate)


def _lane_scan(x, op, ident, reverse):
    lane = lax.broadcasted_iota(jnp.int32, x.shape, 1)
    d = 1
    while d < LANES:
        if reverse:
            moved = jnp.where(lane < LANES - d, pltpu.roll(x, LANES - d, axis=1), ident)
        else:
            moved = jnp.where(lane >= d, pltpu.roll(x, d, axis=1), ident)
        x = op(x, moved)
        d *= 2
    return x


def _log_sigmoid(x):
    return jnp.minimum(x, 0.0) - jnp.log(1.0 + jnp.exp(-jnp.abs(x)))


_ROW_ALPHA, _ROW_AT, _ROW_EM, _ROW_WEND, _ROW_A, _ROW_E = range(6)
STATE_ROWS = V_DIM + 16


def _core_kernel(*refs, seq, context_pass):
    nc = seq // CHUNK
    if context_pass:
        k_ref, vt_ref, gt_ref, cn_out_ref, m_out_ref, row_ref, cn_ref = refs
    else:
        qt_ref, k_ref, vt_ref, og_ref, gt_ref, cn0_ref, m0_ref, hs_ref, row_ref, cn_ref, colb_ref, hacc_ref = refs

    parts = [[], [], [], []]
    for c in range(nc):
        blk = gt_ref[c * CHUNK:(c + 1) * CHUNK, :].T
        for q in range(4):
            parts[q].append(blk[q * HEADS:(q + 1) * HEADS])
    ig = [jnp.concatenate(parts[0], axis=0), jnp.concatenate(parts[1], axis=0)]
    lf = [_log_sigmoid(jnp.concatenate(parts[2], axis=0)), _log_sigmoid(jnp.concatenate(parts[3], axis=0))]
    full = (nc * HEADS, CHUNK)
    betas = []

    for dr in range(2):
        rev = dr == 1
        b = _lane_scan(lf[dr], jnp.add, 0.0, rev)
        edge = 0 if rev else CHUNK - 1
        bl = jnp.broadcast_to(b[:, edge:edge + 1], full)
        g = bl - b + ig[dr]
        gm = jnp.broadcast_to(jnp.max(g, axis=1, keepdims=True), full)
        wend = jnp.exp(g - gm)
        beta = ig[dr] - b
        betas.append(beta)
        cm = _lane_scan(beta, jnp.maximum, -jnp.inf, rev)
        m = jnp.zeros((HEADS, LANES), F32) if context_pass else m0_ref[dr]
        mprev, aa, ee = [None] * nc, [None] * nc, [None] * nc
        for c in (reversed(range(nc)) if rev else range(nc)):
            sl = slice(c * HEADS, (c + 1) * HEADS)
            m_new = jnp.maximum(bl[sl] + m, gm[sl])
            mprev[c] = m
            aa[c] = jnp.exp(bl[sl] + m - m_new)
            ee[c] = jnp.exp(gm[sl] - m_new)
            m = m_new
        if context_pass:
            m_out_ref[dr] = m
        mprev = jnp.concatenate(mprev, axis=0)
        alpha = -jnp.maximum(mprev, cm)
        shape3 = (nc, HEADS, CHUNK)
        row_ref[dr, _ROW_ALPHA] = alpha.reshape(shape3)
        row_ref[dr, _ROW_AT] = jnp.exp(mprev + alpha).reshape(shape3)
        row_ref[dr, _ROW_EM] = jnp.exp(alpha - b).reshape(shape3)
        row_ref[dr, _ROW_WEND] = wend.reshape(shape3)
        row_ref[dr, _ROW_A] = jnp.concatenate(aa, axis=0).reshape(shape3)
        row_ref[dr, _ROW_E] = jnp.concatenate(ee, axis=0).reshape(shape3)

    if context_pass:
        cn_ref[...] = jnp.zeros(cn_ref.shape, F32)
    else:
        cn_ref[...] = cn0_ref[...]
        pad = jnp.zeros((LANES - 2 * HEADS, CHUNK), F32)
        for c in range(nc):
            sl = slice(c * HEADS, (c + 1) * HEADS)
            colb_ref[c] = jnp.concatenate([betas[0][sl], betas[1][sl], pad], axis=0).T

    s_idx = lax.broadcasted_iota(jnp.int32, (CHUNK, CHUNK), 0)
    t_idx = lax.broadcasted_iota(jnp.int32, (CHUNK, CHUNK), 1)
    first_row = lax.broadcasted_iota(jnp.int32, (STATE_ROWS - V_DIM, CHUNK), 0) == 0
    even_lanes = lax.broadcasted_iota(jnp.int32, (1, LANES), 1) < QK_DIM
    zeros_q = jnp.zeros((QK_DIM, CHUNK), BF16)

    def chunk_step(c, dr, second):
        r0 = pl.multiple_of(c * CHUNK, CHUNK)

        def row(qi, h):
            return row_ref[dr, qi, c, h:h + 1, :]

        for p in range(HEADS // 2):
            heads = (2 * p, 2 * p + 1)
            kp = k_ref[pl.ds(r0, CHUNK), p * LANES:(p + 1) * LANES]
            vts = [vt_ref[c, h * V_DIM:(h + 1) * V_DIM, :] for h in heads]
            cn = cn_ref[dr, p]

            if not context_pass:
                qe = qt_ref[c, heads[0] * QK_DIM:(heads[0] + 1) * QK_DIM, :]
                qo = qt_ref[c, heads[1] * QK_DIM:(heads[1] + 1) * QK_DIM, :]
                qblk = jnp.concatenate([jnp.concatenate([qe, zeros_q], axis=1),
                                        jnp.concatenate([zeros_q, qo], axis=1)], axis=0)
                both = _dot(jnp.concatenate([kp, cn.astype(BF16)], axis=0), qblk)
                mask = (s_idx <= t_idx) if dr == 0 else (s_idx >= t_idx)
                for i, h in enumerate(heads):
                    cols = slice(i * CHUNK, (i + 1) * CHUNK)
                    st = both[:CHUNK, cols]
                    inter = both[CHUNK:, cols]
                    beta = colb_ref[c, :, dr * HEADS + h:dr * HEADS + h + 1]
                    pt = st * jnp.where(mask, jnp.exp(beta + row(_ROW_ALPHA, h)), 0.0)
                    at = row(_ROW_AT, h)
                    num = _dot(vts[i], pt.astype(BF16)) + at * inter[:V_DIM]
                    den = jnp.sum(pt, axis=0, keepdims=True) + at * inter[V_DIM:V_DIM + 1]
                    ht = num * (1.0 / jnp.maximum(jnp.abs(den), row(_ROW_EM, h)))
                    hs_ = slice(h * V_DIM, (h + 1) * V_DIM)
                    if not second:
                        hacc_ref[c, hs_, :] = ht
                    else:
                        tot = (hacc_ref[c, hs_, :] + ht).T
                        og = og_ref[pl.ds(r0, CHUNK), hs_].astype(F32)
                        hs_ref[pl.ds(r0, CHUNK), hs_] = (og * tot).astype(BF16)

            lhs = []
            for i, h in enumerate(heads):
                wend = row(_ROW_WEND, h)
                lhs += [vts[i].astype(F32) * wend, jnp.where(first_row, wend, 0.0)]
            cc2 = _dot(jnp.concatenate(lhs, axis=0).astype(BF16), kp)
            cc = jnp.where(even_lanes, cc2[:STATE_ROWS], cc2[STATE_ROWS:])
            a = jnp.where(even_lanes, row(_ROW_A, heads[0]), row(_ROW_A, heads[1]))
            e = jnp.where(even_lanes, row(_ROW_E, heads[0]), row(_ROW_E, heads[1]))
            cn_ref[dr, p] = a * cn + e * cc

    def both_directions(second):
        def body(i, carry):
            chunk_step(i, 0, second)
            chunk_step(nc - 1 - i, 1, second)
            return carry
        return body

    lax.fori_loop(0, nc // 2, both_directions(False), 0)
    lax.fori_loop(nc // 2, nc, both_directions(True), 0)
    if context_pass:
        cn_out_ref[...] = cn_ref[...]


def _core_layer(k, vt, gt, qt=None, og=None, states=None):
    b, seq, _ = k.shape
    d = vt.shape[2]
    nc = seq // CHUNK
    context_pass = states is None
    tok = lambda w: pl.BlockSpec((None, seq, w), lambda i: (i, 0, 0))
    slab = lambda r: pl.BlockSpec((None, nc, r, CHUNK), lambda i: (i, 0, 0, 0))
    cn_spec = pl.BlockSpec((None, 2, HEADS // 2, STATE_ROWS, LANES), lambda i: (i, 0, 0, 0, 0))
    m_spec = pl.BlockSpec((None, 2, HEADS, LANES), lambda i: (i, 0, 0, 0))
    scratch = [pltpu.VMEM((2, 6, nc, HEADS, CHUNK), F32),
               pltpu.VMEM((2, HEADS // 2, STATE_ROWS, LANES), F32)]
    if context_pass:
        ins = (k, vt, gt)
        in_specs = [tok(QK_WIDTH), slab(d), tok(LANES)]
        out_specs = [cn_spec, m_spec]
        out_shape = [jax.ShapeDtypeStruct((b, 2, HEADS // 2, STATE_ROWS, LANES), F32),
                     jax.ShapeDtypeStruct((b, 2, HEADS, LANES), F32)]
    else:
        ins = (qt, k, vt, og, gt) + tuple(states)
        in_specs = [slab(QK_WIDTH), tok(QK_WIDTH), slab(d), tok(d), tok(LANES), cn_spec, m_spec]
        out_specs = tok(d)
        out_shape = jax.ShapeDtypeStruct((b, seq, d), BF16)
        scratch = scratch + [pltpu.VMEM((nc, CHUNK, LANES), F32),
                             pltpu.VMEM((nc, d, CHUNK), F32)]
    return pl.pallas_call(
        functools.partial(_core_kernel, seq=seq, context_pass=context_pass),
        grid=(b,),
        in_specs=in_specs,
        out_specs=out_specs,
        out_shape=out_shape,
        scratch_shapes=scratch,
        compiler_params=_cparams(("parallel",)),
        name="mlstm_core",
    )(*ins)


def _outproj_kernel(x_ref, hs_ref, m_ref, ng_ref, w_ref, o_ref):
    y = _dot(hs_ref[...], w_ref[...])
    o_ref[...] = x_ref[...] + _rms(y, ng_ref[...][1:2] * m_ref[...][2:3])


def _outproj_layer(x, hs, mods, layer, norm_g, w_out, tm):
    b, n, d = x.shape
    tile = pl.BlockSpec((None, tm, d), lambda i, t: (i, t, 0))
    return pl.pallas_call(
        _outproj_kernel,
        grid=(b, n // tm),
        in_specs=[tile, tile, _mod_spec(mods, layer, None), _layer_spec(norm_g, layer), _layer_spec(w_out, 0)],
        out_specs=tile,
        out_shape=jax.ShapeDtypeStruct(x.shape, F32),
        compiler_params=_cparams(("parallel", "parallel")),
        name="mlstm_outproj",
    )(x, hs, mods, norm_g, w_out)


def kernel(x, c, ctx, c_ctx, mod_w, mod_b, norm_g, pool_w, pool_scale, mlstm_w_in, mlstm_b_gate, mlstm_conv_w,
           mlstm_conv_b, mlstm_w_out, ffn_w_in, ffn_conv_w, ffn_conv_b, ffn_w_out):
    b, n, d = x.shape
    ctx_len = ctx.shape[1]
    depth = mod_w.shape[0]
    cond = jnp.zeros((COND_ROWS, d), F32).at[:b].set(c).at[b].set(c_ctx)
    mods = _modulation(cond, mod_w, mod_b).reshape(depth, COND_ROWS, N_MOD, d)
    ctx_row = b
    ctx_flat = ctx.reshape(1, b * ctx_len, d)

    ffn = (ffn_w_in.astype(BF16), ffn_conv_w.reshape(depth, 9, D_FF), ffn_conv_b.reshape(depth, 1, D_FF),
           ffn_w_out.astype(BF16))

    pool = (norm_g, pool_w.astype(BF16), pool_scale.reshape(-1, 1, d))
    x = _pool_layer(x, mods, 0, None, *pool, row_w=GRID_W, tn=512)
    ctx_flat = _pool_layer(ctx_flat, mods, 0, ctx_row, *pool, row_w=ctx_len, tn=512)
    x = _ffn_layer(x, mods, 0, None, norm_g, *ffn, row_w=GRID_W, tm=1024, vertical=True)
    ctx_flat = _ffn_layer(ctx_flat, mods, 0, ctx_row, norm_g, *ffn, row_w=ctx_len, tm=1024, vertical=False)

    qkw = 2 * QK_WIDTH
    w_in = mlstm_w_in.astype(BF16)
    wvt = mlstm_w_in[0, :, qkw:qkw + d].T.astype(BF16)
    wgt = jnp.zeros((d, LANES), F32).at[:, :4 * HEADS].set(mlstm_w_in[0, :, qkw + 2 * d:]).astype(BF16)
    bgate = jnp.zeros((1, LANES), F32).at[0, :4 * HEADS].set(mlstm_b_gate[0])
    proj = (norm_g, w_in, mlstm_conv_w.reshape(-1, 9, qkw), mlstm_conv_b.reshape(-1, 1, qkw), wvt, wgt, bgate)

    _, k_c, vt_c, _, gt_c = _proj_layer(ctx_flat, mods, 1, ctx_row, *proj, row_w=ctx_len, tm=1024, vertical=False)
    states = _core_layer(k_c.reshape(b, ctx_len, QK_WIDTH), vt_c.reshape(b, ctx_len // CHUNK, d, CHUNK),
                         gt_c.reshape(b, ctx_len, LANES))

    qt, k, vt, og, gt = _proj_layer(x, mods, 1, None, *proj, row_w=GRID_W, tm=1024, vertical=True)
    hs = _core_layer(k, vt, gt, qt=qt, og=og, states=states)
    x = _outproj_layer(x, hs, mods, 1, norm_g, mlstm_w_out.astype(BF16), 1024)
    x = _ffn_layer(x, mods, 1, None, norm_g, *ffn, row_w=GRID_W, tm=1024, vertical=True)
    return x
```

```python
import functools

import numpy as np
import jax
import jax.numpy as jnp
from jax import lax
from jax.experimental import pallas as pl
from jax.experimental.pallas import tpu as pltpu

F32 = jnp.float32
BF16 = jnp.bfloat16

D_MODEL = 1024
GRID_W = 64
EPS = 1e-6
N_MOD = 6
POOL_WINDOWS = (2, 4, 8, 16)
POOL_GROUP_DIM = D_MODEL // len(POOL_WINDOWS)
HEADS = 8
V_DIM = D_MODEL // HEADS
QK_DIM = V_DIM // 2
QK_WIDTH = HEADS * QK_DIM
D_FF = ((8 * D_MODEL // 3 + 255) // 256) * 256
COND_ROWS = 16
LANES = 128
CHUNK = LANES
FF_COLS = 256
VMEM_LIMIT = 56 * 1024 * 1024
LOG2_E = 1.4426950408889634


def _cparams(sem):
    return pltpu.CompilerParams(dimension_semantics=sem, vmem_limit_bytes=VMEM_LIMIT)


def _sigmoid(x):
    return 0.5 * jnp.tanh(0.5 * x) + 0.5


def _silu_of_twice(hx):
    return hx + hx * jnp.tanh(hx)


def _rms(xf, g):
    ms = jnp.mean(xf * xf, axis=-1, keepdims=True)
    return xf * lax.rsqrt(ms + EPS) * g


def _modulate(xf, g, shift, scale):
    return _rms(xf, g * (1.0 + scale)) + shift


def _dot(a, b):
    return jnp.dot(a, b, preferred_element_type=F32)


def _layer_spec(a, layer):
    nd = a.ndim - 1
    return pl.BlockSpec((None,) + a.shape[1:], lambda i, t: (layer,) + (0,) * nd, pipeline_mode=pl.Buffered(1))


def _const_spec(a):
    nd = a.ndim
    return pl.BlockSpec(a.shape, lambda i, t: (0,) * nd, pipeline_mode=pl.Buffered(1))


def _mod_spec(mods, layer, mod_row):
    blk = (None, None) + mods.shape[2:]
    if mod_row is None:
        return pl.BlockSpec(blk, lambda i, t: (layer, i, 0, 0))
    return pl.BlockSpec(blk, lambda i, t: (layer, mod_row, 0, 0))


def _tile_specs(d, tm, row_w, vertical, n):
    main = pl.BlockSpec((None, tm, d), lambda i, t: (i, t, 0))
    if not vertical:
        return [main]
    r = tm // row_w
    n_rows = n // row_w
    top = pl.BlockSpec((None, row_w, d), lambda i, t: (i, jnp.maximum(t * r - 1, 0), 0))
    bot = pl.BlockSpec((None, row_w, d), lambda i, t: (i, jnp.minimum((t + 1) * r, n_rows - 1), 0))
    return [main, top, bot]


def _mod_kernel(c_ref, w_ref, b_ref, o_ref):
    s = _silu_of_twice(0.5 * c_ref[...]).astype(BF16)
    o_ref[...] = _dot(s, w_ref[...].astype(BF16)) + b_ref[...]


def _modulation(cond, mod_w, mod_b):
    depth, d, n = mod_w.shape
    tn = 1536
    return pl.pallas_call(
        _mod_kernel,
        grid=(depth, n // tn),
        in_specs=[pl.BlockSpec((COND_ROWS, d), lambda i, j: (0, 0)),
                  pl.BlockSpec((None, d, tn), lambda i, j: (i, 0, j)),
                  pl.BlockSpec((None, 1, tn), lambda i, j: (i, 0, j))],
        out_specs=pl.BlockSpec((None, COND_ROWS, tn), lambda i, j: (i, 0, j)),
        out_shape=jax.ShapeDtypeStruct((depth, COND_ROWS, n), F32),
        compiler_params=_cparams(("parallel", "parallel")),
        name="modulation",
    )(cond, mod_w, mod_b.reshape(depth, 1, n))


def _pool_inv_counts(row_w, tn):
    pos = np.arange(tn) % row_w
    rows = []
    for win in POOL_WINDOWS:
        lo = np.clip(pos - win // 2, 0, row_w)
        hi = np.clip(pos - win // 2 + win, 0, row_w)
        rows.append((1.0 / (hi - lo)).astype(np.float32))
    return jnp.asarray(np.broadcast_to(np.stack(rows)[:, :, None], (len(POOL_WINDOWS), tn, LANES)))


def _pool_kernel(x_ref, m_ref, ng_ref, pw_ref, ps_ref, ic_ref, *rest, row_w):
    n_cast = len(rest) // 2
    o_ref = rest[n_cast]
    for src_ref, dst_ref in zip(rest[:n_cast], rest[n_cast + 1:]):
        dst_ref[...] = src_ref[...].astype(BF16)
    x = x_ref[...]
    m = m_ref[...]
    ng = ng_ref[...]
    h = _modulate(x, ng[0:1], m[0:1], m[1:2])
    tn = x.shape[0]
    gd = POOL_GROUP_DIM
    pos = lax.broadcasted_iota(jnp.int32, (1, row_w, gd), 1)

    def shifted(a, d):
        rolled = pltpu.roll(a, d % tn, axis=0).reshape(tn // row_w, row_w, gd)
        valid = (pos >= d) if d > 0 else (pos < row_w + d)
        return jnp.where(valid, rolled, 0.0).reshape(tn, gd)

    ys = []
    for gi, win in enumerate(POOL_WINDOWS):
        hg = h[:, gi * gd:(gi + 1) * gd]
        half = win // 2
        back, fwd, k = hg, hg, 1
        while k < half:
            back = back + shifted(back, k)
            fwd = fwd + shifted(fwd, -k)
            k *= 2
        total = shifted(back, 1) + fwd
        inv_cnt = ic_ref[gi]
        y = total * jnp.concatenate([inv_cnt] * (gd // LANES), axis=1) - hg
        ys.append(_dot(y.astype(BF16), pw_ref[gi].astype(BF16)))
    y = jnp.concatenate(ys, axis=1) * ps_ref[...]
    o_ref[...] = x + _rms(y, ng[1:2] * m[2:3])


def _pool_layer(x, mods, layer, mod_row, norm_g, pool_w, pool_scale, row_w, tn, cast_weights=()):
    b, n, d = x.shape
    nt = n // tn
    inv_cnt = _pool_inv_counts(row_w, tn)
    tile = pl.BlockSpec((None, tn, d), lambda i, t: (i, t, 0))
    slabs = [pl.BlockSpec((w.shape[0] // (b * nt), w.shape[1]), lambda i, t: (i * nt + t, 0)) for w in cast_weights]
    out = pl.pallas_call(
        functools.partial(_pool_kernel, row_w=row_w),
        grid=(b, nt),
        in_specs=[tile, _mod_spec(mods, layer, mod_row), _layer_spec(norm_g, layer), _layer_spec(pool_w, 0),
                  _layer_spec(pool_scale, 0), _const_spec(inv_cnt)] + slabs,
        out_specs=[tile] + slabs,
        out_shape=[jax.ShapeDtypeStruct(x.shape, F32)] + [jax.ShapeDtypeStruct(w.shape, BF16) for w in cast_weights],
        compiler_params=_cparams(("parallel", "parallel")),
        name="pool_mixer",
    )(x, mods, norm_g, pool_w, pool_scale, inv_cnt, *cast_weights)
    return out if cast_weights else out[0]


def _conv3x3(g, w9, bias, row_w, tm, vertical):
    rows, c = g.shape
    left = pltpu.roll(g, 1, axis=0)
    right = pltpu.roll(g, rows - 1, axis=0)
    pos = lax.broadcasted_iota(jnp.int32, (row_w, c), 0)
    inside = {-1: pos >= 1, 1: pos < row_w - 1}
    acc = None
    for dr in ((-1, 0, 1) if vertical else (0,)):
        off = (row_w if vertical else 0) + dr * row_w
        for dc, arr in ((-1, left), (0, g), (1, right)):
            tap = (dr + 1) * 3 + (dc + 1)
            w = w9[tap:tap + 1]
            if dc:
                w = jnp.where(inside[dc], w, 0.0)
            term = arr[off:off + tm].reshape(tm // row_w, row_w, c) * w[None]
            acc = term if acc is None else acc + term
    return acc.reshape(tm, c) + bias


def _fill_modulated(hh_ref, x, halo_refs, ng_row, shift, scale, row_w, tm):
    gain = ng_row * (1.0 + scale)
    if halo_refs is None:
        hh_ref[...] = (_rms(x, gain) + shift).astype(BF16)
        return
    top_ref, bot_ref = halo_refs
    t = pl.program_id(1)
    last = pl.num_programs(1) - 1
    top = (_rms(top_ref[...], gain) + shift) * (t > 0).astype(F32)
    bot = (_rms(bot_ref[...], gain) + shift) * (t < last).astype(F32)
    hh_ref[0:row_w] = top.astype(BF16)
    hh_ref[row_w:row_w + tm] = (_rms(x, gain) + shift).astype(BF16)
    hh_ref[row_w + tm:row_w + tm + row_w] = bot.astype(BF16)


def _split_tile_refs(refs, vertical):
    if vertical:
        return refs[0], (refs[1], refs[2]), refs[3:]
    return refs[0], None, refs[1:]


def _ffn_kernel(*refs, row_w, tm, vertical):
    x_ref, halo, rest = _split_tile_refs(refs, vertical)
    m_ref, ng_ref, win_ref, cw_ref, cb_ref, wo_ref, o_ref, hh_ref, a_ref = rest
    x = x_ref[...]
    m = m_ref[...]
    ng = ng_ref[...]
    _fill_modulated(hh_ref, x, halo, ng[2:3], m[3:4], m[4:5], row_w, tm)
    lo = row_w if vertical else 0
    for j in range(D_FF // FF_COLS):
        cs = slice(j * FF_COLS, (j + 1) * FF_COLS)
        gate_cs = slice(D_FF + j * FF_COLS, D_FF + (j + 1) * FF_COLS)
        g = _dot(hh_ref[...], win_ref[:, gate_cs])
        half_cv = _conv3x3(g, 0.5 * cw_ref[:, cs], 0.5 * cb_ref[:, cs], row_w, tm, vertical)
        u = _dot(hh_ref[lo:lo + tm], win_ref[:, cs])
        a_ref[:, cs] = (_silu_of_twice(half_cv) * u).astype(BF16)
    f = _dot(a_ref[...], wo_ref[...])
    o_ref[...] = x + _rms(f, ng[3:4] * m[5:6])


def _ffn_layer(x, mods, layer, mod_row, norm_g, w_in, conv_w, conv_b, w_out, row_w, tm, vertical):
    b, n, d = x.shape
    tmh = tm + 2 * row_w if vertical else tm
    x_specs = _tile_specs(d, tm, row_w, vertical, n)
    params = (norm_g, w_in, conv_w, conv_b, w_out)
    return pl.pallas_call(
        functools.partial(_ffn_kernel, row_w=row_w, tm=tm, vertical=vertical),
        grid=(b, n // tm),
        in_specs=x_specs + [_mod_spec(mods, layer, mod_row)] + [_layer_spec(a, layer) for a in params],
        out_specs=pl.BlockSpec((None, tm, d), lambda i, t: (i, t, 0)),
        out_shape=jax.ShapeDtypeStruct(x.shape, F32),
        scratch_shapes=[pltpu.VMEM((tmh, d), BF16), pltpu.VMEM((tm, D_FF), BF16)],
        compiler_params=_cparams(("parallel", "parallel")),
        name="conv_ffn",
    )(*([x] * len(x_specs)), mods, *params)


def _proj_kernel(*refs, row_w, tm, vertical):
    x_ref, halo, rest = _split_tile_refs(refs, vertical)
    (m_ref, ng_ref, win_ref, cw_ref, cb_ref, wvt_ref, wgt_ref, bg_ref,
     qt_ref, k_ref, vt_ref, og_ref, gt_ref, hh_ref) = rest
    m = m_ref[...]
    ng = ng_ref[...]
    _fill_modulated(hh_ref, x_ref[...], halo, ng[0:1], m[0:1], m[1:2], row_w, tm)
    lo = row_w if vertical else 0
    n_chunks = tm // CHUNK
    d = hh_ref.shape[1]
    for j in range(2 * QK_WIDTH // FF_COLS):
        cs = slice(j * FF_COLS, (j + 1) * FF_COLS)
        g = _dot(hh_ref[...], win_ref[:, cs])
        act = _silu_of_twice(_conv3x3(g, 0.5 * cw_ref[:, cs], 0.5 * cb_ref[:, cs], row_w, tm, vertical))
        if j < QK_WIDTH // FF_COLS:
            for ci in range(n_chunks):
                qt_ref[ci, cs, :] = act[ci * CHUNK:(ci + 1) * CHUNK].T.astype(BF16)
        else:
            k_ref[:, j * FF_COLS - QK_WIDTH:(j + 1) * FF_COLS - QK_WIDTH] = (act * QK_DIM ** -0.5).astype(BF16)
    h = hh_ref[lo:lo + tm]
    vt = lax.dot_general(wvt_ref[...], h, (((1,), (1,)), ((), ())), preferred_element_type=F32)
    for ci in range(n_chunks):
        vt_ref[ci] = vt[:, ci * CHUNK:(ci + 1) * CHUNK].astype(BF16)
    o_cols = slice(2 * QK_WIDTH + d, 2 * QK_WIDTH + 2 * d)
    og_ref[...] = _sigmoid(_dot(h, win_ref[:, o_cols])).astype(BF16)
    gt_ref[...] = _dot(h, wgt_ref[...]) + bg_ref[...]


def _proj_layer(x, mods, layer, mod_row, norm_g, w_in, conv_w, conv_b, wvt, wgt, bgate, row_w, tm, vertical):
    b, n, d = x.shape
    tmh = tm + 2 * row_w if vertical else tm
    x_specs = _tile_specs(d, tm, row_w, vertical, n)
    tile = lambda w: pl.BlockSpec((None, tm, w), lambda i, t: (i, t, 0))
    slab = lambda r: pl.BlockSpec((None, tm // CHUNK, r, CHUNK), lambda i, t: (i, t, 0, 0))
    nc = n // CHUNK
    return pl.pallas_call(
        functools.partial(_proj_kernel, row_w=row_w, tm=tm, vertical=vertical),
        grid=(b, n // tm),
        in_specs=(x_specs + [_mod_spec(mods, layer, mod_row), _layer_spec(norm_g, layer)]
                  + [_layer_spec(a, 0) for a in (w_in, conv_w, conv_b)] + [_const_spec(a) for a in (wvt, wgt, bgate)]),
        out_specs=[slab(QK_WIDTH), tile(QK_WIDTH), slab(d), tile(d), tile(LANES)],
        out_shape=[jax.ShapeDtypeStruct((b, nc, QK_WIDTH, CHUNK), BF16), jax.ShapeDtypeStruct((b, n, QK_WIDTH), BF16),
                   jax.ShapeDtypeStruct((b, nc, d, CHUNK), BF16), jax.ShapeDtypeStruct((b, n, d), BF16),
                   jax.ShapeDtypeStruct((b, n, LANES), F32)],
        scratch_shapes=[pltpu.VMEM((tmh, d), BF16)],
        compiler_params=_cparams(("parallel", "parallel")),
        name="mlstm_proj",
    )(*([x] * len(x_specs)), mods, norm_g, w_in, conv_w, conv_b, wvt, wgt, bgate)


def _lane_scan(x, op, ident, reverse):
    lane = lax.broadcasted_iota(jnp.int32, x.shape, 1)
    d = 1
    while d < LANES:
        if reverse:
            moved = jnp.where(lane < LANES - d, pltpu.roll(x, LANES - d, axis=1), ident)
        else:
            moved = jnp.where(lane >= d, pltpu.roll(x, d, axis=1), ident)
        x = op(x, moved)
        d *= 2
    return x


def _log_sigmoid(x):
    return jnp.minimum(x, 0.0) - jnp.log(1.0 + jnp.exp(-jnp.abs(x)))


_ROW_ALPHA, _ROW_AT, _ROW_EM, _ROW_WEND, _ROW_A, _ROW_E = range(6)
STATE_ROWS = V_DIM + 16


def _core_kernel(*refs, seq, context_pass):
    nc = seq // CHUNK
    if context_pass:
        k_ref, vt_ref, gt_ref, cn_out_ref, m_out_ref, row_ref, cn_ref = refs
    else:
        qt_ref, k_ref, vt_ref, og_ref, gt_ref, cn0_ref, m0_ref, hs_ref, row_ref, cn_ref, colb_ref, hacc_ref = refs

    parts = [[], [], [], []]
    for c in range(nc):
        blk = gt_ref[c * CHUNK:(c + 1) * CHUNK, :].T
        for q in range(4):
            parts[q].append(blk[q * HEADS:(q + 1) * HEADS])
    ig = [jnp.concatenate(parts[0], axis=0), jnp.concatenate(parts[1], axis=0)]
    lf = [_log_sigmoid(jnp.concatenate(parts[2], axis=0)), _log_sigmoid(jnp.concatenate(parts[3], axis=0))]
    full = (nc * HEADS, CHUNK)
    betas = []

    for dr in range(2):
        rev = dr == 1
        b = _lane_scan(lf[dr], jnp.add, 0.0, rev)
        edge = 0 if rev else CHUNK - 1
        bl = jnp.broadcast_to(b[:, edge:edge + 1], full)
        g = bl - b + ig[dr]
        gm = jnp.broadcast_to(jnp.max(g, axis=1, keepdims=True), full)
        wend = jnp.exp(g - gm)
        beta = ig[dr] - b
        betas.append(beta)
        cm = _lane_scan(beta, jnp.maximum, -jnp.inf, rev)
        m = jnp.zeros((HEADS, LANES), F32) if context_pass else m0_ref[dr]
        mprev, aa, ee = [None] * nc, [None] * nc, [None] * nc
        for c in (reversed(range(nc)) if rev else range(nc)):
            sl = slice(c * HEADS, (c + 1) * HEADS)
            m_new = jnp.maximum(bl[sl] + m, gm[sl])
            mprev[c] = m
            aa[c] = jnp.exp(bl[sl] + m - m_new)
            ee[c] = jnp.exp(gm[sl] - m_new)
            m = m_new
        if context_pass:
            m_out_ref[dr] = m
        mprev = jnp.concatenate(mprev, axis=0)
        alpha = -jnp.maximum(mprev, cm)
        shape3 = (nc, HEADS, CHUNK)
        row_ref[dr, _ROW_ALPHA] = (alpha * LOG2_E).reshape(shape3)
        row_ref[dr, _ROW_AT] = jnp.exp(mprev + alpha).reshape(shape3)
        row_ref[dr, _ROW_EM] = jnp.exp(alpha - b).reshape(shape3)
        row_ref[dr, _ROW_WEND] = wend.reshape(shape3)
        row_ref[dr, _ROW_A] = jnp.concatenate(aa, axis=0).reshape(shape3)
        row_ref[dr, _ROW_E] = jnp.concatenate(ee, axis=0).reshape(shape3)

    if context_pass:
        cn_ref[...] = jnp.zeros(cn_ref.shape, F32)
    else:
        cn_ref[...] = cn0_ref[...]
        pad = jnp.zeros((LANES - 2 * HEADS, CHUNK), F32)
        for c in range(nc):
            sl = slice(c * HEADS, (c + 1) * HEADS)
            scaled = jnp.concatenate([betas[0][sl], betas[1][sl], pad], axis=0) * LOG2_E
            colb_ref[c] = scaled.T

    s_idx = lax.broadcasted_iota(jnp.int32, (CHUNK, CHUNK), 0)
    t_idx = lax.broadcasted_iota(jnp.int32, (CHUNK, CHUNK), 1)
    first_row = lax.broadcasted_iota(jnp.int32, (STATE_ROWS - V_DIM, CHUNK), 0) == 0
    even_lanes = lax.broadcasted_iota(jnp.int32, (1, LANES), 1) < QK_DIM
    zeros_q = jnp.zeros((QK_DIM, CHUNK), BF16)

    def chunk_step(c, dr, second):
        r0 = pl.multiple_of(c * CHUNK, CHUNK)

        def row(qi, h):
            return row_ref[dr, qi, c, h:h + 1, :]

        for p in range(HEADS // 2):
            heads = (2 * p, 2 * p + 1)
            kp = k_ref[pl.ds(r0, CHUNK), p * LANES:(p + 1) * LANES]
            vts = [vt_ref[c, h * V_DIM:(h + 1) * V_DIM, :] for h in heads]
            cn = cn_ref[dr, p]

            if not context_pass:
                qe = qt_ref[c, heads[0] * QK_DIM:(heads[0] + 1) * QK_DIM, :]
                qo = qt_ref[c, heads[1] * QK_DIM:(heads[1] + 1) * QK_DIM, :]
                qblk = jnp.concatenate([jnp.concatenate([qe, zeros_q], axis=1),
                                        jnp.concatenate([zeros_q, qo], axis=1)], axis=0)
                both = _dot(jnp.concatenate([kp, cn.astype(BF16)], axis=0), qblk)
                mask = (s_idx <= t_idx) if dr == 0 else (s_idx >= t_idx)
                for i, h in enumerate(heads):
                    cols = slice(i * CHUNK, (i + 1) * CHUNK)
                    st = both[:CHUNK, cols]
                    inter = both[CHUNK:, cols]
                    beta = colb_ref[c, :, dr * HEADS + h:dr * HEADS + h + 1]
                    pt = st * jnp.where(mask, jnp.exp2(beta + row(_ROW_ALPHA, h)), 0.0)
                    at = row(_ROW_AT, h)
                    num = _dot(vts[i], pt.astype(BF16)) + at * inter[:V_DIM]
                    den = jnp.sum(pt, axis=0, keepdims=True) + at * inter[V_DIM:V_DIM + 1]
                    ht = num * (1.0 / jnp.maximum(jnp.abs(den), row(_ROW_EM, h)))
                    hs_ = slice(h * V_DIM, (h + 1) * V_DIM)
                    if not second:
                        hacc_ref[c, hs_, :] = ht
                    else:
                        tot = (hacc_ref[c, hs_, :] + ht).T
                        og = og_ref[pl.ds(r0, CHUNK), hs_].astype(F32)
                        hs_ref[pl.ds(r0, CHUNK), hs_] = (og * tot).astype(BF16)

            lhs = []
            for i, h in enumerate(heads):
                wend = row(_ROW_WEND, h)
                lhs += [vts[i].astype(F32) * wend, jnp.where(first_row, wend, 0.0)]
            cc2 = _dot(jnp.concatenate(lhs, axis=0).astype(BF16), kp)
            cc = jnp.where(even_lanes, cc2[:STATE_ROWS], cc2[STATE_ROWS:])
            a = jnp.where(even_lanes, row(_ROW_A, heads[0]), row(_ROW_A, heads[1]))
            e = jnp.where(even_lanes, row(_ROW_E, heads[0]), row(_ROW_E, heads[1]))
            cn_ref[dr, p] = a * cn + e * cc

    def both_directions(second):
        def body(i, carry):
            chunk_step(i, 0, second)
            chunk_step(nc - 1 - i, 1, second)
            return carry
        return body

    lax.fori_loop(0, nc // 2, both_directions(False), 0)
    lax.fori_loop(nc // 2, nc, both_directions(True), 0)
    if context_pass:
        cn_out_ref[...] = cn_ref[...]


def _core_layer(k, vt, gt, qt=None, og=None, states=None):
    b, seq, _ = k.shape
    d = vt.shape[2]
    nc = seq // CHUNK
    context_pass = states is None
    tok = lambda w: pl.BlockSpec((None, seq, w), lambda i: (i, 0, 0))
    slab = lambda r: pl.BlockSpec((None, nc, r, CHUNK), lambda i: (i, 0, 0, 0))
    cn_spec = pl.BlockSpec((None, 2, HEADS // 2, STATE_ROWS, LANES), lambda i: (i, 0, 0, 0, 0))
    m_spec = pl.BlockSpec((None, 2, HEADS, LANES), lambda i: (i, 0, 0, 0))
    scratch = [pltpu.VMEM((2, 6, nc, HEADS, CHUNK), F32),
               pltpu.VMEM((2, HEADS // 2, STATE_ROWS, LANES), F32)]
    if context_pass:
        ins = (k, vt, gt)
        in_specs = [tok(QK_WIDTH), slab(d), tok(LANES)]
        out_specs = [cn_spec, m_spec]
        out_shape = [jax.ShapeDtypeStruct((b, 2, HEADS // 2, STATE_ROWS, LANES), F32),
                     jax.ShapeDtypeStruct((b, 2, HEADS, LANES), F32)]
    else:
        ins = (qt, k, vt, og, gt) + tuple(states)
        in_specs = [slab(QK_WIDTH), tok(QK_WIDTH), slab(d), tok(d), tok(LANES), cn_spec, m_spec]
        out_specs = tok(d)
        out_shape = jax.ShapeDtypeStruct((b, seq, d), BF16)
        scratch = scratch + [pltpu.VMEM((nc, CHUNK, LANES), F32),
                             pltpu.VMEM((nc, d, CHUNK), F32)]
    return pl.pallas_call(
        functools.partial(_core_kernel, seq=seq, context_pass=context_pass),
        grid=(b,),
        in_specs=in_specs,
        out_specs=out_specs,
        out_shape=out_shape,
        scratch_shapes=scratch,
        compiler_params=_cparams(("parallel",)),
        name="mlstm_core",
    )(*ins)


def _outproj_kernel(x_ref, hs_ref, m_ref, ng_ref, w_ref, o_ref):
    y = _dot(hs_ref[...], w_ref[...])
    o_ref[...] = x_ref[...] + _rms(y, ng_ref[...][1:2] * m_ref[...][2:3])


def _outproj_layer(x, hs, mods, layer, norm_g, w_out, tm):
    b, n, d = x.shape
    tile = pl.BlockSpec((None, tm, d), lambda i, t: (i, t, 0))
    return pl.pallas_call(
        _outproj_kernel,
        grid=(b, n // tm),
        in_specs=[tile, tile, _mod_spec(mods, layer, None), _layer_spec(norm_g, layer), _layer_spec(w_out, 0)],
        out_specs=tile,
        out_shape=jax.ShapeDtypeStruct(x.shape, F32),
        compiler_params=_cparams(("parallel", "parallel")),
        name="mlstm_outproj",
    )(x, hs, mods, norm_g, w_out)


def kernel(x, c, ctx, c_ctx, mod_w, mod_b, norm_g, pool_w, pool_scale, mlstm_w_in, mlstm_b_gate, mlstm_conv_w,
           mlstm_conv_b, mlstm_w_out, ffn_w_in, ffn_conv_w, ffn_conv_b, ffn_w_out):
    b, n, d = x.shape
    ctx_len = ctx.shape[1]
    depth = mod_w.shape[0]
    cond = jnp.zeros((COND_ROWS, d), F32).at[:b].set(c).at[b].set(c_ctx)
    mods = _modulation(cond, mod_w, mod_b).reshape(depth, COND_ROWS, N_MOD, d)
    ctx_row = b
    ctx_flat = ctx.reshape(1, b * ctx_len, d)

    pool = (norm_g, pool_w, pool_scale.reshape(-1, 1, d))
    f32_weights = (ffn_w_in.reshape(depth * d, 2 * D_FF), ffn_w_out.reshape(depth * D_FF, d),
                   mlstm_w_in.reshape(-1, mlstm_w_in.shape[-1]))
    x, ffn_in, ffn_out, mlstm_in = _pool_layer(x, mods, 0, None, *pool, row_w=GRID_W, tn=512,
                                               cast_weights=f32_weights)
    ctx_flat = _pool_layer(ctx_flat, mods, 0, ctx_row, *pool, row_w=ctx_len, tn=512)
    ffn = (ffn_in.reshape(ffn_w_in.shape), ffn_conv_w.reshape(depth, 9, D_FF), ffn_conv_b.reshape(depth, 1, D_FF),
           ffn_out.reshape(ffn_w_out.shape))
    x = _ffn_layer(x, mods, 0, None, norm_g, *ffn, row_w=GRID_W, tm=1024, vertical=True)
    ctx_flat = _ffn_layer(ctx_flat, mods, 0, ctx_row, norm_g, *ffn, row_w=ctx_len, tm=1024, vertical=False)

    qkw = 2 * QK_WIDTH
    w_in = mlstm_in.reshape(mlstm_w_in.shape)
    wvt = w_in[0, :, qkw:qkw + d].T
    wgt = jnp.zeros((d, LANES), F32).at[:, :4 * HEADS].set(mlstm_w_in[0, :, qkw + 2 * d:]).astype(BF16)
    bgate = jnp.zeros((1, LANES), F32).at[0, :4 * HEADS].set(mlstm_b_gate[0])
    proj = (norm_g, w_in, mlstm_conv_w.reshape(-1, 9, qkw), mlstm_conv_b.reshape(-1, 1, qkw), wvt, wgt, bgate)

    _, k_c, vt_c, _, gt_c = _proj_layer(ctx_flat, mods, 1, ctx_row, *proj, row_w=ctx_len, tm=1024, vertical=False)
    states = _core_layer(k_c.reshape(b, ctx_len, QK_WIDTH), vt_c.reshape(b, ctx_len // CHUNK, d, CHUNK),
                         gt_c.reshape(b, ctx_len, LANES))

    qt, k, vt, og, gt = _proj_layer(x, mods, 1, None, *proj, row_w=GRID_W, tm=1024, vertical=True)
    hs = _core_layer(k, vt, gt, qt=qt, og=og, states=states)
    x = _outproj_layer(x, hs, mods, 1, norm_g, mlstm_w_out.astype(BF16), 1024)
    x = _ffn_layer(x, mods, 1, None, norm_g, *ffn, row_w=GRID_W, tm=1024, vertical=True)
    return x
```

```python
import functools

import numpy as np
import jax
import jax.numpy as jnp
from jax import lax
from jax.experimental import pallas as pl
from jax.experimental.pallas import tpu as pltpu

F32 = jnp.float32
BF16 = jnp.bfloat16

D_MODEL = 1024
GRID_W = 64
EPS = 1e-6
N_MOD = 6
POOL_WINDOWS = (2, 4, 8, 16)
POOL_GROUP_DIM = D_MODEL // len(POOL_WINDOWS)
HEADS = 8
V_DIM = D_MODEL // HEADS
QK_DIM = V_DIM // 2
QK_WIDTH = HEADS * QK_DIM
D_FF = ((8 * D_MODEL // 3 + 255) // 256) * 256
COND_ROWS = 16
LANES = 128
CHUNK = LANES
FF_COLS = 256
VMEM_LIMIT = 56 * 1024 * 1024
LOG2_E = 1.4426950408889634


def _cparams(sem):
    return pltpu.CompilerParams(dimension_semantics=sem, vmem_limit_bytes=VMEM_LIMIT)


def _sigmoid(x):
    return 0.5 * jnp.tanh(0.5 * x) + 0.5


def _silu_of_twice(hx):
    return hx + hx * jnp.tanh(hx)


def _rms(xf, g):
    ms = jnp.mean(xf * xf, axis=-1, keepdims=True)
    return xf * lax.rsqrt(ms + EPS) * g


def _modulate(xf, g, shift, scale):
    return _rms(xf, g * (1.0 + scale)) + shift


def _dot(a, b):
    return jnp.dot(a, b, preferred_element_type=F32)


def _layer_spec(a, layer):
    nd = a.ndim - 1
    return pl.BlockSpec((None,) + a.shape[1:], lambda i, t: (layer,) + (0,) * nd, pipeline_mode=pl.Buffered(1))


def _const_spec(a):
    nd = a.ndim
    return pl.BlockSpec(a.shape, lambda i, t: (0,) * nd, pipeline_mode=pl.Buffered(1))


def _mod_spec(mods, layer, mod_row):
    blk = (None, None) + mods.shape[2:]
    if mod_row is None:
        return pl.BlockSpec(blk, lambda i, t: (layer, i, 0, 0))
    return pl.BlockSpec(blk, lambda i, t: (layer, mod_row, 0, 0))


def _tile_specs(d, tm, row_w, vertical, n):
    main = pl.BlockSpec((None, tm, d), lambda i, t: (i, t, 0))
    if not vertical:
        return [main]
    r = tm // row_w
    n_rows = n // row_w
    top = pl.BlockSpec((None, row_w, d), lambda i, t: (i, jnp.maximum(t * r - 1, 0), 0))
    bot = pl.BlockSpec((None, row_w, d), lambda i, t: (i, jnp.minimum((t + 1) * r, n_rows - 1), 0))
    return [main, top, bot]


def _cast_specs(jobs, steps, step_of):
    ins, outs, shapes = [], [], []
    for w, layer, n_slabs in jobs:
        blk = (None, w.shape[1] // n_slabs, w.shape[2])
        slab = lambda i, t, n_slabs=n_slabs: step_of(i, t) * n_slabs // steps
        ins.append(pl.BlockSpec(blk, lambda i, t, layer=layer, slab=slab: (layer, slab(i, t), 0)))
        outs.append(pl.BlockSpec(blk, lambda i, t, slab=slab: (0, slab(i, t), 0)))
        shapes.append(jax.ShapeDtypeStruct((1,) + w.shape[1:], BF16))
    return ins, outs, shapes


def _cast_slabs(src_refs, dst_refs):
    for src_ref, dst_ref in zip(src_refs, dst_refs):
        dst_ref[...] = src_ref[...].astype(BF16)


def _mod_kernel(c_ref, w_ref, b_ref, o_ref):
    s = _silu_of_twice(0.5 * c_ref[...]).astype(BF16)
    o_ref[...] = _dot(s, w_ref[...].astype(BF16)) + b_ref[...]


def _modulation(cond, mod_w, mod_b):
    depth, d, n = mod_w.shape
    tn = 1536
    return pl.pallas_call(
        _mod_kernel,
        grid=(depth, n // tn),
        in_specs=[pl.BlockSpec((COND_ROWS, d), lambda i, j: (0, 0)),
                  pl.BlockSpec((None, d, tn), lambda i, j: (i, 0, j)),
                  pl.BlockSpec((None, 1, tn), lambda i, j: (i, 0, j))],
        out_specs=pl.BlockSpec((None, COND_ROWS, tn), lambda i, j: (i, 0, j)),
        out_shape=jax.ShapeDtypeStruct((depth, COND_ROWS, n), F32),
        compiler_params=_cparams(("parallel", "parallel")),
        name="modulation",
    )(cond, mod_w, mod_b.reshape(depth, 1, n))


def _pool_inv_counts(row_w, tn):
    pos = np.arange(tn) % row_w
    rows = []
    for win in POOL_WINDOWS:
        lo = np.clip(pos - win // 2, 0, row_w)
        hi = np.clip(pos - win // 2 + win, 0, row_w)
        rows.append((1.0 / (hi - lo)).astype(np.float32))
    return jnp.asarray(np.broadcast_to(np.stack(rows)[:, :, None], (len(POOL_WINDOWS), tn, LANES)))


def _pool_kernel(x_ref, m_ref, ng_ref, pw_ref, ps_ref, ic_ref, *rest, row_w):
    n_cast = len(rest) // 2
    o_ref = rest[n_cast]
    _cast_slabs(rest[:n_cast], rest[n_cast + 1:])
    x = x_ref[...]
    m = m_ref[...]
    ng = ng_ref[...]
    h = _modulate(x, ng[0:1], m[0:1], m[1:2])
    tn = x.shape[0]
    gd = POOL_GROUP_DIM
    pos = lax.broadcasted_iota(jnp.int32, (1, row_w, gd), 1)

    def shifted(a, d):
        rolled = pltpu.roll(a, d % tn, axis=0).reshape(tn // row_w, row_w, gd)
        valid = (pos >= d) if d > 0 else (pos < row_w + d)
        return jnp.where(valid, rolled, 0.0).reshape(tn, gd)

    ys = []
    for gi, win in enumerate(POOL_WINDOWS):
        hg = h[:, gi * gd:(gi + 1) * gd]
        half = win // 2
        back, fwd, k = hg, hg, 1
        while k < half:
            back = back + shifted(back, k)
            fwd = fwd + shifted(fwd, -k)
            k *= 2
        total = shifted(back, 1) + fwd
        inv_cnt = ic_ref[gi]
        y = total * jnp.concatenate([inv_cnt] * (gd // LANES), axis=1) - hg
        ys.append(_dot(y.astype(BF16), pw_ref[gi].astype(BF16)))
    y = jnp.concatenate(ys, axis=1) * ps_ref[...]
    o_ref[...] = x + _rms(y, ng[1:2] * m[2:3])


def _pool_layer(x, mods, layer, mod_row, norm_g, pool_w, pool_scale, row_w, tn, cast_jobs=()):
    b, n, d = x.shape
    nt = n // tn
    inv_cnt = _pool_inv_counts(row_w, tn)
    tile = pl.BlockSpec((None, tn, d), lambda i, t: (i, t, 0))
    cast_in, cast_out, cast_shapes = _cast_specs(cast_jobs, b * nt, lambda i, t: i * nt + t)
    out = pl.pallas_call(
        functools.partial(_pool_kernel, row_w=row_w),
        grid=(b, nt),
        in_specs=[tile, _mod_spec(mods, layer, mod_row), _layer_spec(norm_g, layer), _layer_spec(pool_w, 0),
                  _layer_spec(pool_scale, 0), _const_spec(inv_cnt)] + cast_in,
        out_specs=[tile] + cast_out,
        out_shape=[jax.ShapeDtypeStruct(x.shape, F32)] + cast_shapes,
        compiler_params=_cparams(("parallel", "arbitrary")),
        name="pool_mixer",
    )(x, mods, norm_g, pool_w, pool_scale, inv_cnt, *[job[0] for job in cast_jobs])
    return out if cast_jobs else out[0]


def _conv3x3(g, w9, bias, row_w, tm, vertical):
    rows, c = g.shape
    left = pltpu.roll(g, 1, axis=0)
    right = pltpu.roll(g, rows - 1, axis=0)
    pos = lax.broadcasted_iota(jnp.int32, (row_w, c), 0)
    inside = {-1: pos >= 1, 1: pos < row_w - 1}
    acc = None
    for dr in ((-1, 0, 1) if vertical else (0,)):
        off = (row_w if vertical else 0) + dr * row_w
        for dc, arr in ((-1, left), (0, g), (1, right)):
            tap = (dr + 1) * 3 + (dc + 1)
            w = w9[tap:tap + 1]
            if dc:
                w = jnp.where(inside[dc], w, 0.0)
            term = arr[off:off + tm].reshape(tm // row_w, row_w, c) * w[None]
            acc = term if acc is None else acc + term
    return acc.reshape(tm, c) + bias


def _fill_modulated(hh_ref, x, halo_refs, ng_row, shift, scale, row_w, tm):
    gain = ng_row * (1.0 + scale)
    if halo_refs is None:
        hh_ref[...] = (_rms(x, gain) + shift).astype(BF16)
        return
    top_ref, bot_ref = halo_refs
    t = pl.program_id(1)
    last = pl.num_programs(1) - 1
    top = (_rms(top_ref[...], gain) + shift) * (t > 0).astype(F32)
    bot = (_rms(bot_ref[...], gain) + shift) * (t < last).astype(F32)
    hh_ref[0:row_w] = top.astype(BF16)
    hh_ref[row_w:row_w + tm] = (_rms(x, gain) + shift).astype(BF16)
    hh_ref[row_w + tm:row_w + tm + row_w] = bot.astype(BF16)


def _split_tile_refs(refs, vertical):
    if vertical:
        return refs[0], (refs[1], refs[2]), refs[3:]
    return refs[0], None, refs[1:]


def _ffn_kernel(*refs, row_w, tm, vertical):
    x_ref, halo, rest = _split_tile_refs(refs, vertical)
    m_ref, ng_ref, win_ref, cw_ref, cb_ref, wo_ref, o_ref, hh_ref, a_ref = rest
    x = x_ref[...]
    m = m_ref[...]
    ng = ng_ref[...]
    _fill_modulated(hh_ref, x, halo, ng[2:3], m[3:4], m[4:5], row_w, tm)
    lo = row_w if vertical else 0
    for j in range(D_FF // FF_COLS):
        cs = slice(j * FF_COLS, (j + 1) * FF_COLS)
        gate_cs = slice(D_FF + j * FF_COLS, D_FF + (j + 1) * FF_COLS)
        g = _dot(hh_ref[...], win_ref[:, gate_cs])
        half_cv = _conv3x3(g, 0.5 * cw_ref[:, cs], 0.5 * cb_ref[:, cs], row_w, tm, vertical)
        u = _dot(hh_ref[lo:lo + tm], win_ref[:, cs])
        a_ref[:, cs] = (_silu_of_twice(half_cv) * u).astype(BF16)
    f = _dot(a_ref[...], wo_ref[...])
    o_ref[...] = x + _rms(f, ng[3:4] * m[5:6])


def _ffn_layer(x, mods, layer, mod_row, norm_g, w_in, conv_w, conv_b, w_out, row_w, tm, vertical):
    b, n, d = x.shape
    tmh = tm + 2 * row_w if vertical else tm
    x_specs = _tile_specs(d, tm, row_w, vertical, n)
    params = ((norm_g, layer), (w_in, 0), (conv_w, layer), (conv_b, layer), (w_out, 0))
    return pl.pallas_call(
        functools.partial(_ffn_kernel, row_w=row_w, tm=tm, vertical=vertical),
        grid=(b, n // tm),
        in_specs=x_specs + [_mod_spec(mods, layer, mod_row)] + [_layer_spec(a, l) for a, l in params],
        out_specs=pl.BlockSpec((None, tm, d), lambda i, t: (i, t, 0)),
        out_shape=jax.ShapeDtypeStruct(x.shape, F32),
        scratch_shapes=[pltpu.VMEM((tmh, d), BF16), pltpu.VMEM((tm, D_FF), BF16)],
        compiler_params=_cparams(("parallel", "parallel")),
        name="conv_ffn",
    )(*([x] * len(x_specs)), mods, *[a for a, _ in params])


def _proj_kernel(*refs, row_w, tm, vertical, n_cast):
    x_ref, halo, rest = _split_tile_refs(refs, vertical)
    m_ref, ng_ref, win_ref, cw_ref, cb_ref, wgt_ref, bg_ref = rest[:7]
    qt_ref, k_ref, vt_ref, og_ref, gt_ref = rest[7 + n_cast:12 + n_cast]
    hh_ref = rest[-1]
    _cast_slabs(rest[7:7 + n_cast], rest[12 + n_cast:12 + 2 * n_cast])
    m = m_ref[...]
    ng = ng_ref[...]
    _fill_modulated(hh_ref, x_ref[...], halo, ng[0:1], m[0:1], m[1:2], row_w, tm)
    lo = row_w if vertical else 0
    n_chunks = tm // CHUNK
    d = hh_ref.shape[1]
    for j in range(2 * QK_WIDTH // FF_COLS):
        cs = slice(j * FF_COLS, (j + 1) * FF_COLS)
        g = _dot(hh_ref[...], win_ref[:, cs])
        act = _silu_of_twice(_conv3x3(g, 0.5 * cw_ref[:, cs], 0.5 * cb_ref[:, cs], row_w, tm, vertical))
        if j < QK_WIDTH // FF_COLS:
            for ci in range(n_chunks):
                qt_ref[ci, cs, :] = act[ci * CHUNK:(ci + 1) * CHUNK].T.astype(BF16)
        else:
            k_ref[:, j * FF_COLS - QK_WIDTH:(j + 1) * FF_COLS - QK_WIDTH] = (act * QK_DIM ** -0.5).astype(BF16)
    h = hh_ref[lo:lo + tm]
    v_cols = slice(2 * QK_WIDTH, 2 * QK_WIDTH + d)
    o_cols = slice(2 * QK_WIDTH + d, 2 * QK_WIDTH + 2 * d)
    v = _dot(h, win_ref[:, v_cols])
    for ci in range(n_chunks):
        vt_ref[ci] = v[ci * CHUNK:(ci + 1) * CHUNK].T.astype(BF16)
    og_ref[...] = _sigmoid(_dot(h, win_ref[:, o_cols])).astype(BF16)
    gt_ref[...] = _dot(h, wgt_ref[...]) + bg_ref[...]


def _proj_layer(x, mods, layer, mod_row, norm_g, w_in, conv_w, conv_b, wgt, bgate, row_w, tm, vertical, cast_jobs=()):
    b, n, d = x.shape
    nt = n // tm
    cast_in, cast_out, cast_shapes = _cast_specs(cast_jobs, b * nt, lambda i, t: i * nt + t)
    tmh = tm + 2 * row_w if vertical else tm
    x_specs = _tile_specs(d, tm, row_w, vertical, n)
    tile = lambda w: pl.BlockSpec((None, tm, w), lambda i, t: (i, t, 0))
    slab = lambda r: pl.BlockSpec((None, tm // CHUNK, r, CHUNK), lambda i, t: (i, t, 0, 0))
    nc = n // CHUNK
    return pl.pallas_call(
        functools.partial(_proj_kernel, row_w=row_w, tm=tm, vertical=vertical, n_cast=len(cast_jobs)),
        grid=(b, nt),
        in_specs=(x_specs + [_mod_spec(mods, layer, mod_row), _layer_spec(norm_g, layer)]
                  + [_layer_spec(a, 0) for a in (w_in, conv_w, conv_b)] + [_const_spec(a) for a in (wgt, bgate)]
                  + cast_in),
        out_specs=[slab(QK_WIDTH), tile(QK_WIDTH), slab(d), tile(d), tile(LANES)] + cast_out,
        out_shape=[jax.ShapeDtypeStruct((b, nc, QK_WIDTH, CHUNK), BF16), jax.ShapeDtypeStruct((b, n, QK_WIDTH), BF16),
                   jax.ShapeDtypeStruct((b, nc, d, CHUNK), BF16), jax.ShapeDtypeStruct((b, n, d), BF16),
                   jax.ShapeDtypeStruct((b, n, LANES), F32)] + cast_shapes,
        scratch_shapes=[pltpu.VMEM((tmh, d), BF16)],
        compiler_params=_cparams(("parallel", "parallel")),
        name="mlstm_proj",
    )(*([x] * len(x_specs)), mods, norm_g, w_in, conv_w, conv_b, wgt, bgate, *[job[0] for job in cast_jobs])


def _lane_scan(x, op, ident, reverse):
    lane = lax.broadcasted_iota(jnp.int32, x.shape, 1)
    d = 1
    while d < LANES:
        if reverse:
            moved = jnp.where(lane < LANES - d, pltpu.roll(x, LANES - d, axis=1), ident)
        else:
            moved = jnp.where(lane >= d, pltpu.roll(x, d, axis=1), ident)
        x = op(x, moved)
        d *= 2
    return x


def _log_sigmoid(x):
    return jnp.minimum(x, 0.0) - jnp.log(1.0 + jnp.exp(-jnp.abs(x)))


_ROW_ALPHA, _ROW_AT, _ROW_EM, _ROW_WEND, _ROW_A, _ROW_E = range(6)
STATE_ROWS = V_DIM + 16


def _core_kernel(*refs, seq, context_pass):
    nc = seq // CHUNK
    if context_pass:
        k_ref, vt_ref, gt_ref, cn_out_ref, m_out_ref, row_ref, cn_ref = refs
    else:
        qt_ref, k_ref, vt_ref, og_ref, gt_ref, cn0_ref, m0_ref, hs_ref, row_ref, cn_ref, colb_ref, hacc_ref = refs

    parts = [[], [], [], []]
    for c in range(nc):
        blk = gt_ref[c * CHUNK:(c + 1) * CHUNK, :].T
        for q in range(4):
            parts[q].append(blk[q * HEADS:(q + 1) * HEADS])
    ig = [jnp.concatenate(parts[0], axis=0), jnp.concatenate(parts[1], axis=0)]
    lf = [_log_sigmoid(jnp.concatenate(parts[2], axis=0)), _log_sigmoid(jnp.concatenate(parts[3], axis=0))]
    full = (nc * HEADS, CHUNK)
    betas = []

    for dr in range(2):
        rev = dr == 1
        b = _lane_scan(lf[dr], jnp.add, 0.0, rev)
        edge = 0 if rev else CHUNK - 1
        bl = jnp.broadcast_to(b[:, edge:edge + 1], full)
        g = bl - b + ig[dr]
        gm = jnp.broadcast_to(jnp.max(g, axis=1, keepdims=True), full)
        wend = jnp.exp(g - gm)
        beta = ig[dr] - b
        betas.append(beta)
        cm = _lane_scan(beta, jnp.maximum, -jnp.inf, rev)
        m = jnp.zeros((HEADS, LANES), F32) if context_pass else m0_ref[dr]
        mprev, aa, ee = [None] * nc, [None] * nc, [None] * nc
        for c in (reversed(range(nc)) if rev else range(nc)):
            sl = slice(c * HEADS, (c + 1) * HEADS)
            m_new = jnp.maximum(bl[sl] + m, gm[sl])
            mprev[c] = m
            aa[c] = jnp.exp(bl[sl] + m - m_new)
            ee[c] = jnp.exp(gm[sl] - m_new)
            m = m_new
        if context_pass:
            m_out_ref[dr] = m
        mprev = jnp.concatenate(mprev, axis=0)
        alpha = -jnp.maximum(mprev, cm)
        shape3 = (nc, HEADS, CHUNK)
        row_ref[dr, _ROW_ALPHA] = (alpha * LOG2_E).reshape(shape3)
        row_ref[dr, _ROW_AT] = jnp.exp(mprev + alpha).reshape(shape3)
        row_ref[dr, _ROW_EM] = jnp.exp(alpha - b).reshape(shape3)
        row_ref[dr, _ROW_WEND] = wend.reshape(shape3)
        row_ref[dr, _ROW_A] = jnp.concatenate(aa, axis=0).reshape(shape3)
        row_ref[dr, _ROW_E] = jnp.concatenate(ee, axis=0).reshape(shape3)

    if context_pass:
        cn_ref[...] = jnp.zeros(cn_ref.shape, F32)
    else:
        cn_ref[...] = cn0_ref[...]
        pad = jnp.zeros((LANES - 2 * HEADS, CHUNK), F32)
        for c in range(nc):
            sl = slice(c * HEADS, (c + 1) * HEADS)
            scaled = jnp.concatenate([betas[0][sl], betas[1][sl], pad], axis=0) * LOG2_E
            colb_ref[c] = scaled.T

    s_idx = lax.broadcasted_iota(jnp.int32, (CHUNK, CHUNK), 0)
    t_idx = lax.broadcasted_iota(jnp.int32, (CHUNK, CHUNK), 1)
    first_row = lax.broadcasted_iota(jnp.int32, (STATE_ROWS - V_DIM, CHUNK), 0) == 0
    even_lanes = lax.broadcasted_iota(jnp.int32, (1, LANES), 1) < QK_DIM
    zeros_q = jnp.zeros((QK_DIM, CHUNK), BF16)

    def chunk_step(c, dr, second):
        r0 = pl.multiple_of(c * CHUNK, CHUNK)

        def row(qi, h):
            return row_ref[dr, qi, c, h:h + 1, :]

        for p in range(HEADS // 2):
            heads = (2 * p, 2 * p + 1)
            kp = k_ref[pl.ds(r0, CHUNK), p * LANES:(p + 1) * LANES]
            vts = [vt_ref[c, h * V_DIM:(h + 1) * V_DIM, :] for h in heads]
            cn = cn_ref[dr, p]

            if not context_pass:
                qe = qt_ref[c, heads[0] * QK_DIM:(heads[0] + 1) * QK_DIM, :]
                qo = qt_ref[c, heads[1] * QK_DIM:(heads[1] + 1) * QK_DIM, :]
                qblk = jnp.concatenate([jnp.concatenate([qe, zeros_q], axis=1),
                                        jnp.concatenate([zeros_q, qo], axis=1)], axis=0)
                both = _dot(jnp.concatenate([kp, cn.astype(BF16)], axis=0), qblk)
                mask = (s_idx <= t_idx) if dr == 0 else (s_idx >= t_idx)
                for i, h in enumerate(heads):
                    cols = slice(i * CHUNK, (i + 1) * CHUNK)
                    st = both[:CHUNK, cols]
                    inter = both[CHUNK:, cols]
                    beta = colb_ref[c, :, dr * HEADS + h:dr * HEADS + h + 1]
                    pt = st * jnp.where(mask, jnp.exp2(beta + row(_ROW_ALPHA, h)), 0.0)
                    at = row(_ROW_AT, h)
                    num = _dot(vts[i], pt.astype(BF16)) + at * inter[:V_DIM]
                    den = jnp.sum(pt, axis=0, keepdims=True) + at * inter[V_DIM:V_DIM + 1]
                    ht = num * (1.0 / jnp.maximum(jnp.abs(den), row(_ROW_EM, h)))
                    hs_ = slice(h * V_DIM, (h + 1) * V_DIM)
                    if not second:
                        hacc_ref[c, hs_, :] = ht
                    else:
                        tot = (hacc_ref[c, hs_, :] + ht).T
                        og = og_ref[pl.ds(r0, CHUNK), hs_].astype(F32)
                        hs_ref[pl.ds(r0, CHUNK), hs_] = (og * tot).astype(BF16)

            lhs = []
            for i, h in enumerate(heads):
                wend = row(_ROW_WEND, h)
                lhs += [vts[i].astype(F32) * wend, jnp.where(first_row, wend, 0.0)]
            cc2 = _dot(jnp.concatenate(lhs, axis=0).astype(BF16), kp)
            cc = jnp.where(even_lanes, cc2[:STATE_ROWS], cc2[STATE_ROWS:])
            a = jnp.where(even_lanes, row(_ROW_A, heads[0]), row(_ROW_A, heads[1]))
            e = jnp.where(even_lanes, row(_ROW_E, heads[0]), row(_ROW_E, heads[1]))
            cn_ref[dr, p] = a * cn + e * cc

    def both_directions(second):
        def body(i, carry):
            chunk_step(i, 0, second)
            chunk_step(nc - 1 - i, 1, second)
            return carry
        return body

    lax.fori_loop(0, nc // 2, both_directions(False), 0)
    lax.fori_loop(nc // 2, nc, both_directions(True), 0)
    if context_pass:
        cn_out_ref[...] = cn_ref[...]


def _core_layer(k, vt, gt, qt=None, og=None, states=None):
    b, seq, _ = k.shape
    d = vt.shape[2]
    nc = seq // CHUNK
    context_pass = states is None
    tok = lambda w: pl.BlockSpec((None, seq, w), lambda i: (i, 0, 0))
    slab = lambda r: pl.BlockSpec((None, nc, r, CHUNK), lambda i: (i, 0, 0, 0))
    cn_spec = pl.BlockSpec((None, 2, HEADS // 2, STATE_ROWS, LANES), lambda i: (i, 0, 0, 0, 0))
    m_spec = pl.BlockSpec((None, 2, HEADS, LANES), lambda i: (i, 0, 0, 0))
    scratch = [pltpu.VMEM((2, 6, nc, HEADS, CHUNK), F32),
               pltpu.VMEM((2, HEADS // 2, STATE_ROWS, LANES), F32)]
    if context_pass:
        ins = (k, vt, gt)
        in_specs = [tok(QK_WIDTH), slab(d), tok(LANES)]
        out_specs = [cn_spec, m_spec]
        out_shape = [jax.ShapeDtypeStruct((b, 2, HEADS // 2, STATE_ROWS, LANES), F32),
                     jax.ShapeDtypeStruct((b, 2, HEADS, LANES), F32)]
    else:
        ins = (qt, k, vt, og, gt) + tuple(states)
        in_specs = [slab(QK_WIDTH), tok(QK_WIDTH), slab(d), tok(d), tok(LANES), cn_spec, m_spec]
        out_specs = tok(d)
        out_shape = jax.ShapeDtypeStruct((b, seq, d), BF16)
        scratch = scratch + [pltpu.VMEM((nc, CHUNK, LANES), F32),
                             pltpu.VMEM((nc, d, CHUNK), F32)]
    return pl.pallas_call(
        functools.partial(_core_kernel, seq=seq, context_pass=context_pass),
        grid=(b,),
        in_specs=in_specs,
        out_specs=out_specs,
        out_shape=out_shape,
        scratch_shapes=scratch,
        compiler_params=_cparams(("parallel",)),
        name="mlstm_core",
    )(*ins)


def _outproj_kernel(x_ref, hs_ref, m_ref, ng_ref, w_ref, o_ref):
    y = _dot(hs_ref[...], w_ref[...])
    o_ref[...] = x_ref[...] + _rms(y, ng_ref[...][1:2] * m_ref[...][2:3])


def _outproj_layer(x, hs, mods, layer, norm_g, w_out, tm):
    b, n, d = x.shape
    tile = pl.BlockSpec((None, tm, d), lambda i, t: (i, t, 0))
    return pl.pallas_call(
        _outproj_kernel,
        grid=(b, n // tm),
        in_specs=[tile, tile, _mod_spec(mods, layer, None), _layer_spec(norm_g, layer), _layer_spec(w_out, 0)],
        out_specs=tile,
        out_shape=jax.ShapeDtypeStruct(x.shape, F32),
        compiler_params=_cparams(("parallel", "parallel")),
        name="mlstm_outproj",
    )(x, hs, mods, norm_g, w_out)


def kernel(x, c, ctx, c_ctx, mod_w, mod_b, norm_g, pool_w, pool_scale, mlstm_w_in, mlstm_b_gate, mlstm_conv_w,
           mlstm_conv_b, mlstm_w_out, ffn_w_in, ffn_conv_w, ffn_conv_b, ffn_w_out):
    b, n, d = x.shape
    ctx_len = ctx.shape[1]
    depth = mod_w.shape[0]
    cond = jnp.zeros((COND_ROWS, d), F32).at[:b].set(c).at[b].set(c_ctx)
    mods = _modulation(cond, mod_w, mod_b).reshape(depth, COND_ROWS, N_MOD, d)
    ctx_row = b
    ctx_flat = ctx.reshape(1, b * ctx_len, d)

    conv_w = ffn_conv_w.reshape(depth, 9, D_FF)
    conv_b = ffn_conv_b.reshape(depth, 1, D_FF)
    pool = (norm_g, pool_w, pool_scale.reshape(-1, 1, d))
    x, w_in0, w_out0, w_in = _pool_layer(x, mods, 0, None, *pool, row_w=GRID_W, tn=512,
                                         cast_jobs=((ffn_w_in, 0, 32), (ffn_w_out, 0, 16), (mlstm_w_in, 0, 32)))
    ctx_flat = _pool_layer(ctx_flat, mods, 0, ctx_row, *pool, row_w=ctx_len, tn=512)
    x = _ffn_layer(x, mods, 0, None, norm_g, w_in0, conv_w, conv_b, w_out0, row_w=GRID_W, tm=1024, vertical=True)
    ctx_flat = _ffn_layer(ctx_flat, mods, 0, ctx_row, norm_g, w_in0, conv_w, conv_b, w_out0, row_w=ctx_len, tm=1024,
                          vertical=False)

    qkw = 2 * QK_WIDTH
    wgt = jnp.pad(w_in[0, :, qkw + 2 * d:], ((0, 0), (0, LANES - 4 * HEADS)))
    bgate = jnp.zeros((1, LANES), F32).at[0, :4 * HEADS].set(mlstm_b_gate[0])
    proj = (norm_g, w_in, mlstm_conv_w.reshape(-1, 9, qkw), mlstm_conv_b.reshape(-1, 1, qkw), wgt, bgate)

    _, k_c, vt_c, _, gt_c = _proj_layer(ctx_flat, mods, 1, ctx_row, *proj, row_w=ctx_len, tm=1024, vertical=False)
    states = _core_layer(k_c.reshape(b, ctx_len, QK_WIDTH), vt_c.reshape(b, ctx_len // CHUNK, d, CHUNK),
                         gt_c.reshape(b, ctx_len, LANES))

    qt, k, vt, og, gt, w_in1, w_out1 = _proj_layer(x, mods, 1, None, *proj, row_w=GRID_W, tm=1024, vertical=True,
                                                   cast_jobs=((ffn_w_in, 1, 16), (ffn_w_out, 1, 16)))
    hs = _core_layer(k, vt, gt, qt=qt, og=og, states=states)
    x = _outproj_layer(x, hs, mods, 1, norm_g, mlstm_w_out.astype(BF16), 1024)
    return _ffn_layer(x, mods, 1, None, norm_g, w_in1, conv_w, conv_b, w_out1, row_w=GRID_W, tm=1024, vertical=True)
```

```python
import functools

import numpy as np
import jax
import jax.numpy as jnp
from jax import lax
from jax.experimental import pallas as pl
from jax.experimental.pallas import tpu as pltpu

F32 = jnp.float32
BF16 = jnp.bfloat16

D_MODEL = 1024
GRID_W = 64
EPS = 1e-6
N_MOD = 6
POOL_WINDOWS = (2, 4, 8, 16)
POOL_GROUP_DIM = D_MODEL // len(POOL_WINDOWS)
HEADS = 8
V_DIM = D_MODEL // HEADS
QK_DIM = V_DIM // 2
QK_WIDTH = HEADS * QK_DIM
D_FF = ((8 * D_MODEL // 3 + 255) // 256) * 256
COND_ROWS = 16
LANES = 128
CHUNK = LANES
FF_COLS = 256
VMEM_LIMIT = 56 * 1024 * 1024
LOG2_E = 1.4426950408889634


def _cparams(sem):
    return pltpu.CompilerParams(dimension_semantics=sem, vmem_limit_bytes=VMEM_LIMIT)


def _sigmoid(x):
    return 0.5 * jnp.tanh(0.5 * x) + 0.5


def _silu_of_twice(hx):
    return hx + hx * jnp.tanh(hx)


def _rms(xf, g):
    ms = jnp.mean(xf * xf, axis=-1, keepdims=True)
    return xf * lax.rsqrt(ms + EPS) * g


def _modulate(xf, g, shift, scale):
    return _rms(xf, g * (1.0 + scale)) + shift


def _dot(a, b):
    return jnp.dot(a, b, preferred_element_type=F32)


def _dot_nt(a, b):
    return lax.dot_general(a, b, (((1,), (1,)), ((), ())), preferred_element_type=F32)


def _layer_spec(a, layer):
    nd = a.ndim - 1
    return pl.BlockSpec((None,) + a.shape[1:], lambda i, t: (layer,) + (0,) * nd, pipeline_mode=pl.Buffered(1))


def _const_spec(a):
    nd = a.ndim
    return pl.BlockSpec(a.shape, lambda i, t: (0,) * nd, pipeline_mode=pl.Buffered(1))


def _mod_spec(mods, layer, mod_row):
    blk = (None, None) + mods.shape[2:]
    if mod_row is None:
        return pl.BlockSpec(blk, lambda i, t: (layer, i, 0, 0))
    return pl.BlockSpec(blk, lambda i, t: (layer, mod_row, 0, 0))


def _tile_specs(d, tm, row_w, vertical, n):
    main = pl.BlockSpec((None, tm, d), lambda i, t: (i, t, 0))
    if not vertical:
        return [main]
    r = tm // row_w
    n_rows = n // row_w
    top = pl.BlockSpec((None, row_w, d), lambda i, t: (i, jnp.maximum(t * r - 1, 0), 0))
    bot = pl.BlockSpec((None, row_w, d), lambda i, t: (i, jnp.minimum((t + 1) * r, n_rows - 1), 0))
    return [main, top, bot]


def _cast_specs(jobs, steps, step_of):
    ins, outs, shapes = [], [], []
    for w, layer, n_slabs, axis in jobs:
        blk = (None,) + tuple(s // n_slabs if a == axis else s for a, s in enumerate(w.shape) if a)
        slab = lambda i, t, n_slabs=n_slabs: step_of(i, t) * n_slabs // steps
        pos = (lambda s: (s, 0)) if axis == 1 else (lambda s: (0, s))
        ins.append(pl.BlockSpec(blk, lambda i, t, layer=layer, slab=slab, pos=pos: (layer,) + pos(slab(i, t))))
        outs.append(pl.BlockSpec(blk, lambda i, t, slab=slab, pos=pos: (0,) + pos(slab(i, t))))
        shapes.append(jax.ShapeDtypeStruct((1,) + w.shape[1:], BF16))
    return ins, outs, shapes


def _cast_slabs(src_refs, dst_refs):
    for src_ref, dst_ref in zip(src_refs, dst_refs):
        dst_ref[...] = src_ref[...].astype(BF16)


def _mod_kernel(c_ref, w_ref, b_ref, o_ref):
    s = _silu_of_twice(0.5 * c_ref[...]).astype(BF16)
    o_ref[...] = _dot(s, w_ref[...].astype(BF16)) + b_ref[...]


def _modulation(cond, mod_w, mod_b):
    depth, d, n = mod_w.shape
    tn = 1536
    return pl.pallas_call(
        _mod_kernel,
        grid=(depth, n // tn),
        in_specs=[pl.BlockSpec((COND_ROWS, d), lambda i, j: (0, 0)),
                  pl.BlockSpec((None, d, tn), lambda i, j: (i, 0, j)),
                  pl.BlockSpec((None, 1, tn), lambda i, j: (i, 0, j))],
        out_specs=pl.BlockSpec((None, COND_ROWS, tn), lambda i, j: (i, 0, j)),
        out_shape=jax.ShapeDtypeStruct((depth, COND_ROWS, n), F32),
        compiler_params=_cparams(("parallel", "parallel")),
        name="modulation",
    )(cond, mod_w, mod_b.reshape(depth, 1, n))


def _pool_inv_counts(row_w, tn):
    pos = np.arange(tn) % row_w
    rows = []
    for win in POOL_WINDOWS:
        lo = np.clip(pos - win // 2, 0, row_w)
        hi = np.clip(pos - win // 2 + win, 0, row_w)
        rows.append((1.0 / (hi - lo)).astype(np.float32))
    return jnp.asarray(np.broadcast_to(np.stack(rows)[:, :, None], (len(POOL_WINDOWS), tn, LANES)))


def _pool_kernel(x_ref, m_ref, ng_ref, pw_ref, ps_ref, ic_ref, *rest, row_w):
    n_cast = len(rest) // 2
    o_ref = rest[n_cast]
    _cast_slabs(rest[:n_cast], rest[n_cast + 1:])
    x = x_ref[...]
    m = m_ref[...]
    ng = ng_ref[...]
    h = _modulate(x, ng[0:1], m[0:1], m[1:2])
    tn = x.shape[0]
    gd = POOL_GROUP_DIM
    pos = lax.broadcasted_iota(jnp.int32, (1, row_w, gd), 1)

    def shifted(a, d):
        rolled = pltpu.roll(a, d % tn, axis=0).reshape(tn // row_w, row_w, gd)
        valid = (pos >= d) if d > 0 else (pos < row_w + d)
        return jnp.where(valid, rolled, 0.0).reshape(tn, gd)

    ys = []
    for gi, win in enumerate(POOL_WINDOWS):
        hg = h[:, gi * gd:(gi + 1) * gd]
        half = win // 2
        back, fwd, k = hg, hg, 1
        while k < half:
            back = back + shifted(back, k)
            fwd = fwd + shifted(fwd, -k)
            k *= 2
        total = shifted(back, 1) + fwd
        inv_cnt = ic_ref[gi]
        y = total * jnp.concatenate([inv_cnt] * (gd // LANES), axis=1) - hg
        ys.append(_dot(y.astype(BF16), pw_ref[gi].astype(BF16)))
    y = jnp.concatenate(ys, axis=1) * ps_ref[...]
    o_ref[...] = x + _rms(y, ng[1:2] * m[2:3])


def _pool_layer(x, mods, layer, mod_row, norm_g, pool_w, pool_scale, row_w, tn, cast_jobs=()):
    b, n, d = x.shape
    nt = n // tn
    inv_cnt = _pool_inv_counts(row_w, tn)
    tile = pl.BlockSpec((None, tn, d), lambda i, t: (i, t, 0))
    cast_in, cast_out, cast_shapes = _cast_specs(cast_jobs, b * nt, lambda i, t: i * nt + t)
    out = pl.pallas_call(
        functools.partial(_pool_kernel, row_w=row_w),
        grid=(b, nt),
        in_specs=[tile, _mod_spec(mods, layer, mod_row), _layer_spec(norm_g, layer), _layer_spec(pool_w, 0),
                  _layer_spec(pool_scale, 0), _const_spec(inv_cnt)] + cast_in,
        out_specs=[tile] + cast_out,
        out_shape=[jax.ShapeDtypeStruct(x.shape, F32)] + cast_shapes,
        compiler_params=_cparams(("parallel", "arbitrary")),
        name="pool_mixer",
    )(x, mods, norm_g, pool_w, pool_scale, inv_cnt, *[job[0] for job in cast_jobs])
    return out if cast_jobs else out[0]


def _conv3x3(g, w9, bias, row_w, tm, vertical):
    rows, c = g.shape
    left = pltpu.roll(g, 1, axis=0)
    right = pltpu.roll(g, rows - 1, axis=0)
    pos = lax.broadcasted_iota(jnp.int32, (row_w, c), 0)
    inside = {-1: pos >= 1, 1: pos < row_w - 1}
    acc = None
    for dr in ((-1, 0, 1) if vertical else (0,)):
        off = (row_w if vertical else 0) + dr * row_w
        for dc, arr in ((-1, left), (0, g), (1, right)):
            tap = (dr + 1) * 3 + (dc + 1)
            w = w9[tap:tap + 1]
            if dc:
                w = jnp.where(inside[dc], w, 0.0)
            term = arr[off:off + tm].reshape(tm // row_w, row_w, c) * w[None]
            acc = term if acc is None else acc + term
    return acc.reshape(tm, c) + bias


def _fill_modulated(hh_ref, x, halo_refs, ng_row, shift, scale, row_w, tm):
    gain = ng_row * (1.0 + scale)
    if halo_refs is None:
        hh_ref[...] = (_rms(x, gain) + shift).astype(BF16)
        return
    top_ref, bot_ref = halo_refs
    t = pl.program_id(1)
    last = pl.num_programs(1) - 1
    top = (_rms(top_ref[...], gain) + shift) * (t > 0).astype(F32)
    bot = (_rms(bot_ref[...], gain) + shift) * (t < last).astype(F32)
    hh_ref[0:row_w] = top.astype(BF16)
    hh_ref[row_w:row_w + tm] = (_rms(x, gain) + shift).astype(BF16)
    hh_ref[row_w + tm:row_w + tm + row_w] = bot.astype(BF16)


def _split_tile_refs(refs, vertical):
    if vertical:
        return refs[0], (refs[1], refs[2]), refs[3:]
    return refs[0], None, refs[1:]


def _ffn_kernel(*refs, row_w, tm, vertical):
    x_ref, halo, rest = _split_tile_refs(refs, vertical)
    m_ref, ng_ref, win_ref, cw_ref, cb_ref, wo_ref, o_ref, hh_ref, a_ref = rest
    x = x_ref[...]
    m = m_ref[...]
    ng = ng_ref[...]
    _fill_modulated(hh_ref, x, halo, ng[2:3], m[3:4], m[4:5], row_w, tm)
    lo = row_w if vertical else 0
    for j in range(D_FF // FF_COLS):
        cs = slice(j * FF_COLS, (j + 1) * FF_COLS)
        gate_cs = slice(D_FF + j * FF_COLS, D_FF + (j + 1) * FF_COLS)
        g = _dot(hh_ref[...], win_ref[:, gate_cs])
        half_cv = _conv3x3(g, 0.5 * cw_ref[:, cs], 0.5 * cb_ref[:, cs], row_w, tm, vertical)
        u = _dot(hh_ref[lo:lo + tm], win_ref[:, cs])
        a_ref[:, cs] = (_silu_of_twice(half_cv) * u).astype(BF16)
    f = _dot(a_ref[...], wo_ref[...])
    o_ref[...] = x + _rms(f, ng[3:4] * m[5:6])


def _ffn_layer(x, mods, layer, mod_row, norm_g, w_in, conv_w, conv_b, w_out, row_w, tm, vertical):
    b, n, d = x.shape
    tmh = tm + 2 * row_w if vertical else tm
    x_specs = _tile_specs(d, tm, row_w, vertical, n)
    params = ((norm_g, layer), (w_in, 0), (conv_w, layer), (conv_b, layer), (w_out, 0))
    return pl.pallas_call(
        functools.partial(_ffn_kernel, row_w=row_w, tm=tm, vertical=vertical),
        grid=(b, n // tm),
        in_specs=x_specs + [_mod_spec(mods, layer, mod_row)] + [_layer_spec(a, l) for a, l in params],
        out_specs=pl.BlockSpec((None, tm, d), lambda i, t: (i, t, 0)),
        out_shape=jax.ShapeDtypeStruct(x.shape, F32),
        scratch_shapes=[pltpu.VMEM((tmh, d), BF16), pltpu.VMEM((tm, D_FF), BF16)],
        compiler_params=_cparams(("parallel", "parallel")),
        name="conv_ffn",
    )(*([x] * len(x_specs)), mods, *[a for a, _ in params])


def _proj_kernel(*refs, row_w, tm, vertical, n_cast):
    x_ref, halo, rest = _split_tile_refs(refs, vertical)
    m_ref, ng_ref, win_ref, cw_ref, cb_ref, wgt_ref, bg_ref = rest[:7]
    qt_ref, k_ref, vt_ref, og_ref, gt_ref = rest[7 + n_cast:12 + n_cast]
    hh_ref = rest[-1]
    _cast_slabs(rest[7:7 + n_cast], rest[12 + n_cast:12 + 2 * n_cast])
    m = m_ref[...]
    ng = ng_ref[...]
    _fill_modulated(hh_ref, x_ref[...], halo, ng[0:1], m[0:1], m[1:2], row_w, tm)
    lo = row_w if vertical else 0
    n_chunks = tm // CHUNK
    d = hh_ref.shape[1]
    for j in range(2 * QK_WIDTH // FF_COLS):
        cs = slice(j * FF_COLS, (j + 1) * FF_COLS)
        g = _dot_nt(hh_ref[...], win_ref[cs, :])
        act = _silu_of_twice(_conv3x3(g, 0.5 * cw_ref[:, cs], 0.5 * cb_ref[:, cs], row_w, tm, vertical))
        if j < QK_WIDTH // FF_COLS:
            for ci in range(n_chunks):
                qt_ref[ci, cs, :] = act[ci * CHUNK:(ci + 1) * CHUNK].T.astype(BF16)
        else:
            k_ref[:, j * FF_COLS - QK_WIDTH:(j + 1) * FF_COLS - QK_WIDTH] = (act * QK_DIM ** -0.5).astype(BF16)
    h = hh_ref[lo:lo + tm]
    v_rows = slice(2 * QK_WIDTH, 2 * QK_WIDTH + d)
    o_rows = slice(2 * QK_WIDTH + d, 2 * QK_WIDTH + 2 * d)
    vt = _dot_nt(win_ref[v_rows, :], h)
    for ci in range(n_chunks):
        vt_ref[ci] = vt[:, ci * CHUNK:(ci + 1) * CHUNK].astype(BF16)
    og_ref[...] = _sigmoid(_dot_nt(h, win_ref[o_rows, :])).astype(BF16)
    gt_ref[...] = _dot_nt(h, wgt_ref[...]) + bg_ref[...]


def _proj_layer(x, mods, layer, mod_row, norm_g, w_in_t, conv_w, conv_b, wgt_t, bgate, row_w, tm, vertical,
                cast_jobs=()):
    b, n, d = x.shape
    nt = n // tm
    cast_in, cast_out, cast_shapes = _cast_specs(cast_jobs, b * nt, lambda i, t: i * nt + t)
    tmh = tm + 2 * row_w if vertical else tm
    x_specs = _tile_specs(d, tm, row_w, vertical, n)
    tile = lambda w: pl.BlockSpec((None, tm, w), lambda i, t: (i, t, 0))
    slab = lambda r: pl.BlockSpec((None, tm // CHUNK, r, CHUNK), lambda i, t: (i, t, 0, 0))
    nc = n // CHUNK
    return pl.pallas_call(
        functools.partial(_proj_kernel, row_w=row_w, tm=tm, vertical=vertical, n_cast=len(cast_jobs)),
        grid=(b, nt),
        in_specs=(x_specs + [_mod_spec(mods, layer, mod_row), _layer_spec(norm_g, layer)]
                  + [_layer_spec(a, 0) for a in (w_in_t, conv_w, conv_b)] + [_const_spec(a) for a in (wgt_t, bgate)]
                  + cast_in),
        out_specs=[slab(QK_WIDTH), tile(QK_WIDTH), slab(d), tile(d), tile(LANES)] + cast_out,
        out_shape=[jax.ShapeDtypeStruct((b, nc, QK_WIDTH, CHUNK), BF16), jax.ShapeDtypeStruct((b, n, QK_WIDTH), BF16),
                   jax.ShapeDtypeStruct((b, nc, d, CHUNK), BF16), jax.ShapeDtypeStruct((b, n, d), BF16),
                   jax.ShapeDtypeStruct((b, n, LANES), F32)] + cast_shapes,
        scratch_shapes=[pltpu.VMEM((tmh, d), BF16)],
        compiler_params=_cparams(("parallel", "parallel")),
        name="mlstm_proj",
    )(*([x] * len(x_specs)), mods, norm_g, w_in_t, conv_w, conv_b, wgt_t, bgate, *[job[0] for job in cast_jobs])


def _lane_scan(x, op, ident, reverse):
    lane = lax.broadcasted_iota(jnp.int32, x.shape, 1)
    d = 1
    while d < LANES:
        if reverse:
            moved = jnp.where(lane < LANES - d, pltpu.roll(x, LANES - d, axis=1), ident)
        else:
            moved = jnp.where(lane >= d, pltpu.roll(x, d, axis=1), ident)
        x = op(x, moved)
        d *= 2
    return x


def _log_sigmoid(x):
    return jnp.minimum(x, 0.0) - jnp.log(1.0 + jnp.exp(-jnp.abs(x)))


_ROW_ALPHA, _ROW_AT, _ROW_EM, _ROW_WEND, _ROW_A, _ROW_E = range(6)
STATE_ROWS = V_DIM + 16


def _core_kernel(*refs, seq, context_pass):
    nc = seq // CHUNK
    if context_pass:
        k_ref, vt_ref, gt_ref, cn_out_ref, m_out_ref, row_ref, cn_ref = refs
    else:
        qt_ref, k_ref, vt_ref, og_ref, gt_ref, cn0_ref, m0_ref, hs_ref, row_ref, cn_ref, colb_ref, hacc_ref = refs

    parts = [[], [], [], []]
    for c in range(nc):
        blk = gt_ref[c * CHUNK:(c + 1) * CHUNK, :].T
        for q in range(4):
            parts[q].append(blk[q * HEADS:(q + 1) * HEADS])
    ig = [jnp.concatenate(parts[0], axis=0), jnp.concatenate(parts[1], axis=0)]
    lf = [_log_sigmoid(jnp.concatenate(parts[2], axis=0)), _log_sigmoid(jnp.concatenate(parts[3], axis=0))]
    full = (nc * HEADS, CHUNK)
    betas = []

    for dr in range(2):
        rev = dr == 1
        b = _lane_scan(lf[dr], jnp.add, 0.0, rev)
        edge = 0 if rev else CHUNK - 1
        bl = jnp.broadcast_to(b[:, edge:edge + 1], full)
        g = bl - b + ig[dr]
        gm = jnp.broadcast_to(jnp.max(g, axis=1, keepdims=True), full)
        wend = jnp.exp(g - gm)
        beta = ig[dr] - b
        betas.append(beta)
        cm = _lane_scan(beta, jnp.maximum, -jnp.inf, rev)
        m = jnp.zeros((HEADS, LANES), F32) if context_pass else m0_ref[dr]
        mprev, aa, ee = [None] * nc, [None] * nc, [None] * nc
        for c in (reversed(range(nc)) if rev else range(nc)):
            sl = slice(c * HEADS, (c + 1) * HEADS)
            m_new = jnp.maximum(bl[sl] + m, gm[sl])
            mprev[c] = m
            aa[c] = jnp.exp(bl[sl] + m - m_new)
            ee[c] = jnp.exp(gm[sl] - m_new)
            m = m_new
        if context_pass:
            m_out_ref[dr] = m
        mprev = jnp.concatenate(mprev, axis=0)
        alpha = -jnp.maximum(mprev, cm)
        shape3 = (nc, HEADS, CHUNK)
        row_ref[dr, _ROW_ALPHA] = (alpha * LOG2_E).reshape(shape3)
        row_ref[dr, _ROW_AT] = jnp.exp(mprev + alpha).reshape(shape3)
        row_ref[dr, _ROW_EM] = jnp.exp(alpha - b).reshape(shape3)
        row_ref[dr, _ROW_WEND] = wend.reshape(shape3)
        row_ref[dr, _ROW_A] = jnp.concatenate(aa, axis=0).reshape(shape3)
        row_ref[dr, _ROW_E] = jnp.concatenate(ee, axis=0).reshape(shape3)

    if context_pass:
        cn_ref[...] = jnp.zeros(cn_ref.shape, F32)
    else:
        cn_ref[...] = cn0_ref[...]
        pad = jnp.zeros((LANES - 2 * HEADS, CHUNK), F32)
        for c in range(nc):
            sl = slice(c * HEADS, (c + 1) * HEADS)
            scaled = jnp.concatenate([betas[0][sl], betas[1][sl], pad], axis=0) * LOG2_E
            colb_ref[c] = scaled.T

    s_idx = lax.broadcasted_iota(jnp.int32, (CHUNK, CHUNK), 0)
    t_idx = lax.broadcasted_iota(jnp.int32, (CHUNK, CHUNK), 1)
    first_row = lax.broadcasted_iota(jnp.int32, (STATE_ROWS - V_DIM, CHUNK), 0) == 0
    even_lanes = lax.broadcasted_iota(jnp.int32, (1, LANES), 1) < QK_DIM
    zeros_q = jnp.zeros((QK_DIM, CHUNK), BF16)

    def chunk_step(c, dr, second):
        r0 = pl.multiple_of(c * CHUNK, CHUNK)

        def row(qi, h):
            return row_ref[dr, qi, c, h:h + 1, :]

        for p in range(HEADS // 2):
            heads = (2 * p, 2 * p + 1)
            kp = k_ref[pl.ds(r0, CHUNK), p * LANES:(p + 1) * LANES]
            vts = [vt_ref[c, h * V_DIM:(h + 1) * V_DIM, :] for h in heads]
            cn = cn_ref[dr, p]

            if not context_pass:
                qe = qt_ref[c, heads[0] * QK_DIM:(heads[0] + 1) * QK_DIM, :]
                qo = qt_ref[c, heads[1] * QK_DIM:(heads[1] + 1) * QK_DIM, :]
                qblk = jnp.concatenate([jnp.concatenate([qe, zeros_q], axis=1),
                                        jnp.concatenate([zeros_q, qo], axis=1)], axis=0)
                both = _dot(jnp.concatenate([kp, cn.astype(BF16)], axis=0), qblk)
                mask = (s_idx <= t_idx) if dr == 0 else (s_idx >= t_idx)
                for i, h in enumerate(heads):
                    cols = slice(i * CHUNK, (i + 1) * CHUNK)
                    st = both[:CHUNK, cols]
                    inter = both[CHUNK:, cols]
                    beta = colb_ref[c, :, dr * HEADS + h:dr * HEADS + h + 1]
                    pt = st * jnp.where(mask, jnp.exp2(beta + row(_ROW_ALPHA, h)), 0.0)
                    at = row(_ROW_AT, h)
                    num = _dot(vts[i], pt.astype(BF16)) + at * inter[:V_DIM]
                    den = jnp.sum(pt, axis=0, keepdims=True) + at * inter[V_DIM:V_DIM + 1]
                    ht = num * (1.0 / jnp.maximum(jnp.abs(den), row(_ROW_EM, h)))
                    hs_ = slice(h * V_DIM, (h + 1) * V_DIM)
                    if not second:
                        hacc_ref[c, hs_, :] = ht
                    else:
                        tot = (hacc_ref[c, hs_, :] + ht).T
                        og = og_ref[pl.ds(r0, CHUNK), hs_].astype(F32)
                        hs_ref[pl.ds(r0, CHUNK), hs_] = (og * tot).astype(BF16)

            lhs = []
            for i, h in enumerate(heads):
                wend = row(_ROW_WEND, h)
                lhs += [vts[i].astype(F32) * wend, jnp.where(first_row, wend, 0.0)]
            cc2 = _dot(jnp.concatenate(lhs, axis=0).astype(BF16), kp)
            cc = jnp.where(even_lanes, cc2[:STATE_ROWS], cc2[STATE_ROWS:])
            a = jnp.where(even_lanes, row(_ROW_A, heads[0]), row(_ROW_A, heads[1]))
            e = jnp.where(even_lanes, row(_ROW_E, heads[0]), row(_ROW_E, heads[1]))
            cn_ref[dr, p] = a * cn + e * cc

    def both_directions(second):
        def body(i, carry):
            chunk_step(i, 0, second)
            chunk_step(nc - 1 - i, 1, second)
            return carry
        return body

    unroll = 2 if nc % 4 == 0 else 1
    lax.fori_loop(0, nc // 2, both_directions(False), 0, unroll=unroll)
    lax.fori_loop(nc // 2, nc, both_directions(True), 0, unroll=unroll)
    if context_pass:
        cn_out_ref[...] = cn_ref[...]


def _core_layer(k, vt, gt, qt=None, og=None, states=None):
    b, seq, _ = k.shape
    d = vt.shape[2]
    nc = seq // CHUNK
    context_pass = states is None
    tok = lambda w: pl.BlockSpec((None, seq, w), lambda i: (i, 0, 0))
    slab = lambda r: pl.BlockSpec((None, nc, r, CHUNK), lambda i: (i, 0, 0, 0))
    cn_spec = pl.BlockSpec((None, 2, HEADS // 2, STATE_ROWS, LANES), lambda i: (i, 0, 0, 0, 0))
    m_spec = pl.BlockSpec((None, 2, HEADS, LANES), lambda i: (i, 0, 0, 0))
    scratch = [pltpu.VMEM((2, 6, nc, HEADS, CHUNK), F32),
               pltpu.VMEM((2, HEADS // 2, STATE_ROWS, LANES), F32)]
    if context_pass:
        ins = (k, vt, gt)
        in_specs = [tok(QK_WIDTH), slab(d), tok(LANES)]
        out_specs = [cn_spec, m_spec]
        out_shape = [jax.ShapeDtypeStruct((b, 2, HEADS // 2, STATE_ROWS, LANES), F32),
                     jax.ShapeDtypeStruct((b, 2, HEADS, LANES), F32)]
    else:
        ins = (qt, k, vt, og, gt) + tuple(states)
        in_specs = [slab(QK_WIDTH), tok(QK_WIDTH), slab(d), tok(d), tok(LANES), cn_spec, m_spec]
        out_specs = tok(d)
        out_shape = jax.ShapeDtypeStruct((b, seq, d), BF16)
        scratch = scratch + [pltpu.VMEM((nc, CHUNK, LANES), F32),
                             pltpu.VMEM((nc, d, CHUNK), F32)]
    return pl.pallas_call(
        functools.partial(_core_kernel, seq=seq, context_pass=context_pass),
        grid=(b,),
        in_specs=in_specs,
        out_specs=out_specs,
        out_shape=out_shape,
        scratch_shapes=scratch,
        compiler_params=_cparams(("parallel",)),
        name="mlstm_core",
    )(*ins)


def _outproj_kernel(x_ref, hs_ref, m_ref, ng_ref, w_ref, o_ref):
    y = _dot(hs_ref[...], w_ref[...])
    o_ref[...] = x_ref[...] + _rms(y, ng_ref[...][1:2] * m_ref[...][2:3])


def _outproj_layer(x, hs, mods, layer, norm_g, w_out, tm):
    b, n, d = x.shape
    tile = pl.BlockSpec((None, tm, d), lambda i, t: (i, t, 0))
    return pl.pallas_call(
        _outproj_kernel,
        grid=(b, n // tm),
        in_specs=[tile, tile, _mod_spec(mods, layer, None), _layer_spec(norm_g, layer), _layer_spec(w_out, 0)],
        out_specs=tile,
        out_shape=jax.ShapeDtypeStruct(x.shape, F32),
        compiler_params=_cparams(("parallel", "parallel")),
        name="mlstm_outproj",
    )(x, hs, mods, norm_g, w_out)


def kernel(x, c, ctx, c_ctx, mod_w, mod_b, norm_g, pool_w, pool_scale, mlstm_w_in, mlstm_b_gate, mlstm_conv_w,
           mlstm_conv_b, mlstm_w_out, ffn_w_in, ffn_conv_w, ffn_conv_b, ffn_w_out):
    b, n, d = x.shape
    ctx_len = ctx.shape[1]
    depth = mod_w.shape[0]
    cond = jnp.zeros((COND_ROWS, d), F32).at[:b].set(c).at[b].set(c_ctx)
    mods = _modulation(cond, mod_w, mod_b).reshape(depth, COND_ROWS, N_MOD, d)
    ctx_row = b
    ctx_flat = ctx.reshape(1, b * ctx_len, d)

    w_in_t = jnp.swapaxes(mlstm_w_in, 1, 2)
    conv_w = ffn_conv_w.reshape(depth, 9, D_FF)
    conv_b = ffn_conv_b.reshape(depth, 1, D_FF)
    pool = (norm_g, pool_w, pool_scale.reshape(-1, 1, d))
    x, w_in0, w_out0, w_in_t = _pool_layer(x, mods, 0, None, *pool, row_w=GRID_W, tn=512,
                                           cast_jobs=((ffn_w_in, 0, 32, 1), (ffn_w_out, 0, 16, 1), (w_in_t, 0, 8, 2)))
    ctx_flat = _pool_layer(ctx_flat, mods, 0, ctx_row, *pool, row_w=ctx_len, tn=512)
    x = _ffn_layer(x, mods, 0, None, norm_g, w_in0, conv_w, conv_b, w_out0, row_w=GRID_W, tm=1024, vertical=True)
    ctx_flat = _ffn_layer(ctx_flat, mods, 0, ctx_row, norm_g, w_in0, conv_w, conv_b, w_out0, row_w=ctx_len, tm=1024,
                          vertical=False)

    qkw = 2 * QK_WIDTH
    wgt_t = jnp.pad(w_in_t[0, qkw + 2 * d:, :], ((0, LANES - 4 * HEADS), (0, 0)))
    bgate = jnp.zeros((1, LANES), F32).at[0, :4 * HEADS].set(mlstm_b_gate[0])
    proj = (norm_g, w_in_t, mlstm_conv_w.reshape(-1, 9, qkw), mlstm_conv_b.reshape(-1, 1, qkw), wgt_t, bgate)

    _, k_c, vt_c, _, gt_c = _proj_layer(ctx_flat, mods, 1, ctx_row, *proj, row_w=ctx_len, tm=1024, vertical=False)
    states = _core_layer(k_c.reshape(b, ctx_len, QK_WIDTH), vt_c.reshape(b, ctx_len // CHUNK, d, CHUNK),
                         gt_c.reshape(b, ctx_len, LANES))

    qt, k, vt, og, gt, w_in1, w_out1 = _proj_layer(x, mods, 1, None, *proj, row_w=GRID_W, tm=1024, vertical=True,
                                                   cast_jobs=((ffn_w_in, 1, 16, 1), (ffn_w_out, 1, 16, 1)))
    hs = _core_layer(k, vt, gt, qt=qt, og=og, states=states)
    x = _outproj_layer(x, hs, mods, 1, norm_g, mlstm_w_out.astype(BF16), 1024)
    return _ffn_layer(x, mods, 1, None, norm_g, w_in1, conv_w, conv_b, w_out1, row_w=GRID_W, tm=1024, vertical=True)
```
